```python
import math
import jax, jax.numpy as jnp
from jax import lax
import numpy as np

D_MODEL = 1024
BATCH = 8
SEQ = 4096
DEPTH = 1
DEC_BATCH = 32
DEC_SEQ = 4
PAST_LEN = 16384
PAGE_SIZE = 128

ATTN_WIDTH = D_MODEL // 2
HEAD_DIM = 64
N_HEADS = ATTN_WIDTH // HEAD_DIM
SSM_WIDTH = D_MODEL - ATTN_WIDTH
SSM_GROUP = 16
N_SSM_GROUPS = SSM_WIDTH // SSM_GROUP
SSM_STATE = 64
PROJ_WIDTH = 3 * ATTN_WIDTH + N_HEADS + SSM_WIDTH
Q_BLOCK = 128
N_EXPERTS = 32
TOP_K = 4
D_FF = D_MODEL
SWIGLU_ALPHA = 1.702
SWIGLU_LIMIT = 7.0
MOE_BLOCK = 128
PLE_DIM = 256
RMS_EPS = 1e-6
STEP_MIN = 1e-3
STEP_MAX = 1e-1

kernel_name = 'hymba_fox_s5_moe_decode_step'


def _rmsnorm(x, g):
    xf = x.astype(jnp.float32)
    y = xf * lax.rsqrt(jnp.mean(xf * xf, axis=-1, keepdims=True) + RMS_EPS)
    return (y * g.astype(jnp.float32)).astype(x.dtype)


def _fox_attention(q, k, v, c_q, c_k, q_pos, k_pos):
    b, sq, h, dh = q.shape
    qb = Q_BLOCK if sq % Q_BLOCK == 0 else sq
    nb = sq // qb
    q_blocks = jnp.swapaxes(q.reshape(b, nb, qb, h, dh), 0, 1)
    cq_blocks = jnp.swapaxes(c_q.reshape(b, nb, qb, h), 0, 1)
    pos_blocks = q_pos.reshape(nb, qb)
    c_k_t = jnp.swapaxes(c_k, 1, 2)
    scale = HEAD_DIM ** -0.5

    def one_block(args):
        qblk, cq, qp = args
        s = jnp.einsum('bqhd,bkhd->bhqk', qblk, k).astype(jnp.float32) * scale
        s = s + jnp.swapaxes(cq, 1, 2)[..., None] - c_k_t[:, :, None, :]
        causal = k_pos[None, :] <= qp[:, None]
        s = jnp.where(causal[None, None], s, -jnp.inf)
        w = jax.nn.softmax(s, axis=-1)
        return jnp.einsum('bhqk,bkhd->bqhd', w.astype(v.dtype), v)

    out = lax.map(one_block, (q_blocks, cq_blocks, pos_blocks))
    return jnp.swapaxes(out, 0, 1).reshape(b, sq, h, dh)


def _complex_affine_combine(e1, e2):
    a1r, a1i, b1r, b1i = e1
    a2r, a2i, b2r, b2i = e2
    return (a2r * a1r - a2i * a1i,
            a2r * a1i + a2i * a1r,
            a2r * b1r - a2i * b1i + b2r,
            a2r * b1i + a2i * b1r + b2i)


def _s5_ssm(u, h0_re, h0_im, lam_re, lam_im, log_step, b_re, b_im, c_re, c_im, d_skip):
    bsz, s, _ = u.shape
    f32 = jnp.float32
    ug = u.reshape(bsz, s, N_SSM_GROUPS, SSM_GROUP).astype(f32)
    lr, li = lam_re.astype(f32), lam_im.astype(f32)
    step = jnp.exp(log_step.astype(f32))[:, None]
    mag = jnp.exp(lr * step)
    abar_re, abar_im = mag * jnp.cos(li * step), mag * jnp.sin(li * step)
    nr, ni = abar_re - 1.0, abar_im
    den = lr * lr + li * li
    coef_re = (nr * lr + ni * li) / den
    coef_im = (ni * lr - nr * li) / den
    br, bi = b_re.astype(f32), b_im.astype(f32)
    bbar_re = coef_re[..., None] * br - coef_im[..., None] * bi
    bbar_im = coef_re[..., None] * bi + coef_im[..., None] * br
    bu_re = jnp.einsum('bsgc,gpc->bsgp', ug, bbar_re)
    bu_im = jnp.einsum('bsgc,gpc->bsgp', ug, bbar_im)
    h0r, h0i = h0_re.astype(f32), h0_im.astype(f32)
    bu_re = bu_re.at[:, 0].add(abar_re * h0r - abar_im * h0i)
    bu_im = bu_im.at[:, 0].add(abar_re * h0i + abar_im * h0r)
    a_re = jnp.broadcast_to(abar_re, (1, s, N_SSM_GROUPS, SSM_STATE))
    a_im = jnp.broadcast_to(abar_im, (1, s, N_SSM_GROUPS, SSM_STATE))
    _, _, h_re, h_im = lax.associative_scan(_complex_affine_combine, (a_re, a_im, bu_re, bu_im), axis=1)
    y = (jnp.einsum('bsgp,gcp->bsgc', h_re, c_re.astype(f32))
         - jnp.einsum('bsgp,gcp->bsgc', h_im, c_im.astype(f32))
         + d_skip.astype(f32) * ug)
    return y.reshape(bsz, s, SSM_WIDTH).astype(u.dtype), h_re[:, -1], h_im[:, -1]


def _moe(h, w_router, b_router, w_up, b_up, w_down, b_down):
    t = h.shape[0]
    tk = t * TOP_K
    logits = (h @ w_router).astype(jnp.float32) + b_router.astype(jnp.float32)
    top_vals, top_idx = lax.top_k(logits, TOP_K)
    gates = jax.nn.softmax(top_vals, axis=-1)
    flat_e = top_idx.reshape(-1).astype(jnp.int32)
    flat_tok = jnp.repeat(jnp.arange(t, dtype=jnp.int32), TOP_K)
    flat_g = gates.reshape(-1)
    order = jnp.argsort(flat_e, stable=True)
    sorted_e, sorted_tok, sorted_g = flat_e[order], flat_tok[order], flat_g[order]
    counts = jnp.bincount(flat_e, length=N_EXPERTS)
    padded = (counts + MOE_BLOCK - 1) // MOE_BLOCK * MOE_BLOCK
    start = jnp.cumsum(counts) - counts
    pend = jnp.cumsum(padded)
    pstart = pend - padded
    dest = pstart[sorted_e] + (jnp.arange(tk) - start[sorted_e])
    n_blocks = -(-tk // MOE_BLOCK) + N_EXPERTS
    block_expert = jnp.minimum(jnp.searchsorted(pend, jnp.arange(n_blocks) * MOE_BLOCK, side='right'),
                               N_EXPERTS - 1).astype(jnp.int32)
    row_tok = jnp.zeros((n_blocks * MOE_BLOCK,), jnp.int32).at[dest].set(sorted_tok)
    xs = h[row_tok].reshape(n_blocks, MOE_BLOCK, h.shape[-1])

    def expert_block(args):
        xb, e = args
        up = xb @ w_up[e] + b_up[e]
        gate = jnp.minimum(up[:, :D_FF], SWIGLU_LIMIT)
        lin = jnp.clip(up[:, D_FF:], -SWIGLU_LIMIT, SWIGLU_LIMIT)
        act = gate * jax.nn.sigmoid(SWIGLU_ALPHA * gate) * (lin + 1.0)
        return act @ w_down[e] + b_down[e]

    out = lax.map(expert_block, (xs, block_expert)).reshape(n_blocks * MOE_BLOCK, -1)
    contrib = out[dest] * sorted_g[:, None].astype(out.dtype)
    return jnp.zeros((t, out.shape[-1]), out.dtype).at[sorted_tok].add(contrib)


def _layer(x, p, past_k, past_v, past_logf, h0_re, h0_im, lw):
    b, s, d = x.shape
    past = past_k.shape[1]
    f32 = jnp.float32
    h = _rmsnorm(x, lw['g_mix'])
    z = h @ lw['w_in']
    q = z[..., :ATTN_WIDTH].reshape(b, s, N_HEADS, HEAD_DIM)
    k = z[..., ATTN_WIDTH:2 * ATTN_WIDTH].reshape(b, s, N_HEADS, HEAD_DIM)
    v = z[..., 2 * ATTN_WIDTH:3 * ATTN_WIDTH].reshape(b, s, N_HEADS, HEAD_DIM)
    f_logit = z[..., 3 * ATTN_WIDTH:3 * ATTN_WIDTH + N_HEADS]
    u = z[..., 3 * ATTN_WIDTH + N_HEADS:]
    logf = jax.nn.log_sigmoid(f_logit.astype(f32) + lw['b_fgate'].astype(f32))
    k_all = jnp.concatenate([past_k.astype(k.dtype), k], axis=1)
    v_all = jnp.concatenate([past_v.astype(v.dtype), v], axis=1)
    c_all = jnp.cumsum(jnp.concatenate([past_logf.astype(f32), logf], axis=1), axis=1)
    k_pos = jnp.arange(past + s)
    q_pos = past + jnp.arange(s)
    attn = _fox_attention(q, k_all, v_all, c_all[:, past:], c_all, q_pos, k_pos).reshape(b, s, ATTN_WIDTH)
    ssm, hT_re, hT_im = _s5_ssm(u, h0_re, h0_im, lw['lam_re'], lw['lam_im'], lw['log_step'],
                                lw['b_ssm_re'], lw['b_ssm_im'], lw['c_ssm_re'], lw['c_ssm_im'], lw['d_ssm'])
    ssm = jax.nn.gelu(ssm)
    ssm = ssm * jax.nn.sigmoid(ssm @ lw['w_glu'] + lw['b_glu'])
    mixed = jnp.concatenate([_rmsnorm(attn, lw['g_attn_out']), _rmsnorm(ssm, lw['g_ssm_out'])], axis=-1)
    x = x + mixed @ lw['w_out']
    hm = _rmsnorm(x, lw['g_ffn']).reshape(b * s, d)
    x = x + _moe(hm, lw['w_router'], lw['b_router'], lw['w_up'], lw['b_up'],
                 lw['w_down'], lw['b_down']).reshape(b, s, d).astype(x.dtype)
    gate = jax.nn.sigmoid(_rmsnorm(x, lw['g_ple']) @ lw['w_ple_gate'])
    x = x + ((p @ lw['w_ple_proj']) * gate).astype(x.dtype)
    return x, k, v, logf, hT_re, hT_im


def setup_inputs(seed: int = 0) -> dict:
    key = jax.random.key(seed)
    keys = iter(jax.random.split(key, 48))
    f32 = jnp.float32

    def nrm(shape, scale):
        return scale * jax.random.normal(next(keys), shape, f32)

    n_pages = PAST_LEN // PAGE_SIZE
    n_used = DEC_BATCH * n_pages
    n_phys = n_used + max(1, n_used // 4)
    page_table = jax.random.permutation(next(keys), n_phys)[:n_used].reshape(DEC_BATCH, n_pages).astype(jnp.int32)
    G, P, C = N_SSM_GROUPS, SSM_STATE, SSM_GROUP
    ns = jnp.arange(P, dtype=f32)
    w_in = jnp.concatenate([nrm((DEPTH, D_MODEL, 3 * ATTN_WIDTH), D_MODEL ** -0.5),
                            nrm((DEPTH, D_MODEL, N_HEADS), 0.5 * D_MODEL ** -0.5),
                            nrm((DEPTH, D_MODEL, SSM_WIDTH), D_MODEL ** -0.5)], axis=-1)
    return {
        'x_prompt': nrm((BATCH, SEQ, D_MODEL), 1.0),
        'x_sample': nrm((DEC_BATCH, DEC_SEQ, D_MODEL), 1.0),
        'p_prompt': nrm((DEPTH, BATCH, SEQ, PLE_DIM), 1.0),
        'p_sample': nrm((DEPTH, DEC_BATCH, DEC_SEQ, PLE_DIM), 1.0),
        'cache_k': nrm((DEPTH, n_phys, PAGE_SIZE, N_HEADS, HEAD_DIM), 1.0),
        'cache_v': nrm((DEPTH, n_phys, PAGE_SIZE, N_HEADS, HEAD_DIM), 1.0),
        'cache_logf': jax.nn.log_sigmoid(2.0 + nrm((DEPTH, n_phys, PAGE_SIZE, N_HEADS), 0.5)),
        'state_ssm_re': nrm((DEPTH, DEC_BATCH, G, P), 0.3),
        'state_ssm_im': nrm((DEPTH, DEC_BATCH, G, P), 0.3),
        'page_table': page_table,
        'g_mix': 1.0 + nrm((DEPTH, D_MODEL), 0.05),
        'w_in': w_in,
        'b_fgate': 2.0 + nrm((DEPTH, N_HEADS), 0.1),
        'lam_re': -0.5 + nrm((DEPTH, G, P), 0.01),
        'lam_im': jnp.pi * ns + nrm((DEPTH, G, P), 0.01),
        'log_step': jax.random.uniform(next(keys), (DEPTH, G), f32, math.log(STEP_MIN), math.log(STEP_MAX)),
        'b_ssm_re': nrm((DEPTH, G, P, C), (2 * C) ** -0.5),
        'b_ssm_im': nrm((DEPTH, G, P, C), (2 * C) ** -0.5),
        'c_ssm_re': nrm((DEPTH, G, C, P), (2 * P) ** -0.5),
        'c_ssm_im': nrm((DEPTH, G, C, P), (2 * P) ** -0.5),
        'd_ssm': nrm((DEPTH, G, C), 0.5),
        'w_glu': nrm((DEPTH, SSM_WIDTH, SSM_WIDTH), SSM_WIDTH ** -0.5),
        'b_glu': nrm((DEPTH, SSM_WIDTH), 0.01),
        'g_attn_out': 1.0 + nrm((DEPTH, ATTN_WIDTH), 0.05),
        'g_ssm_out': 1.0 + nrm((DEPTH, SSM_WIDTH), 0.05),
        'w_out': nrm((DEPTH, D_MODEL, D_MODEL), D_MODEL ** -0.5),
        'g_ffn': 1.0 + nrm((DEPTH, D_MODEL), 0.05),
        'w_router': nrm((DEPTH, D_MODEL, N_EXPERTS), D_MODEL ** -0.5),
        'b_router': nrm((DEPTH, N_EXPERTS), 0.01),
        'w_up': nrm((DEPTH, N_EXPERTS, D_MODEL, 2 * D_FF), D_MODEL ** -0.5),
        'b_up': nrm((DEPTH, N_EXPERTS, 2 * D_FF), 0.01),
        'w_down': nrm((DEPTH, N_EXPERTS, D_FF, D_MODEL), D_FF ** -0.5),
        'b_down': nrm((DEPTH, N_EXPERTS, D_MODEL), 0.01),
        'g_ple': 1.0 + nrm((DEPTH, D_MODEL), 0.05),
        'w_ple_gate': nrm((DEPTH, D_MODEL, D_MODEL), D_MODEL ** -0.5),
        'w_ple_proj': nrm((DEPTH, PLE_DIM, D_MODEL), PLE_DIM ** -0.5),
        'g_final': 1.0 + nrm((D_MODEL,), 0.05),
    }


def reference(x_prompt, x_sample, p_prompt, p_sample, cache_k, cache_v, cache_logf, state_ssm_re, state_ssm_im,
              page_table, g_mix, w_in, b_fgate, lam_re, lam_im, log_step, b_ssm_re, b_ssm_im, c_ssm_re, c_ssm_im,
              d_ssm, w_glu, b_glu, g_attn_out, g_ssm_out, w_out, g_ffn, w_router, b_router, w_up, b_up, w_down,
              b_down, g_ple, w_ple_gate, w_ple_proj, g_final):
    bp = x_prompt.shape[0]
    bd = x_sample.shape[0]
    n_pages = page_table.shape[1]
    past_len = n_pages * cache_k.shape[2]
    xp, xs = x_prompt, x_sample
    kp_l, vp_l, fp_l, hrp_l, hip_l = [], [], [], [], []
    ks_l, vs_l, fs_l, hrs_l, his_l = [], [], [], [], []
    for i in range(DEPTH):
        lw = dict(g_mix=g_mix[i], w_in=w_in[i], b_fgate=b_fgate[i], lam_re=lam_re[i], lam_im=lam_im[i],
                  log_step=log_step[i], b_ssm_re=b_ssm_re[i], b_ssm_im=b_ssm_im[i], c_ssm_re=c_ssm_re[i],
                  c_ssm_im=c_ssm_im[i], d_ssm=d_ssm[i], w_glu=w_glu[i], b_glu=b_glu[i],
                  g_attn_out=g_attn_out[i], g_ssm_out=g_ssm_out[i], w_out=w_out[i], g_ffn=g_ffn[i],
                  w_router=w_router[i], b_router=b_router[i], w_up=w_up[i], b_up=b_up[i], w_down=w_down[i],
                  b_down=b_down[i], g_ple=g_ple[i], w_ple_gate=w_ple_gate[i], w_ple_proj=w_ple_proj[i])
        empty_kv = jnp.zeros((bp, 0, N_HEADS, HEAD_DIM), xp.dtype)
        empty_f = jnp.zeros((bp, 0, N_HEADS), jnp.float32)
        zero_h = jnp.zeros((bp, N_SSM_GROUPS, SSM_STATE), jnp.float32)
        xp, kp, vp, fp, hrp, hip = _layer(xp, p_prompt[i], empty_kv, empty_kv, empty_f, zero_h, zero_h, lw)
        past_k = cache_k[i][page_table].reshape(bd, past_len, N_HEADS, HEAD_DIM)
        past_v = cache_v[i][page_table].reshape(bd, past_len, N_HEADS, HEAD_DIM)
        past_f = cache_logf[i][page_table].reshape(bd, past_len, N_HEADS)
        xs, ks, vs, fs, hrs, his = _layer(xs, p_sample[i], past_k, past_v, past_f,
                                          state_ssm_re[i], state_ssm_im[i], lw)
        kp_l.append(kp); vp_l.append(vp); fp_l.append(fp); hrp_l.append(hrp); hip_l.append(hip)
        ks_l.append(ks); vs_l.append(vs); fs_l.append(fs); hrs_l.append(hrs); his_l.append(his)
    y_prompt = _rmsnorm(xp, g_final)
    y_sample = _rmsnorm(xs, g_final)
    return (y_prompt, y_sample,
            jnp.stack(kp_l), jnp.stack(vp_l), jnp.stack(fp_l), jnp.stack(hrp_l), jnp.stack(hip_l),
            jnp.stack(ks_l), jnp.stack(vs_l), jnp.stack(fs_l), jnp.stack(hrs_l), jnp.stack(his_l))
```

```python
import functools

import jax
import jax.numpy as jnp
from jax import lax
from jax.experimental import pallas as pl
from jax.experimental.pallas import tpu as pltpu

F32 = jnp.float32
BF16 = jnp.bfloat16
I32 = jnp.int32

RMS_EPS = 1e-6
HEAD_DIM = 64
HEADS_PER_LANE_TILE = 2
LANES = 128
SUBLANES = 8
SSM_GROUP = 16
SSM_STATE = 64
SSM_LANE_BLOCKS = 4
N_EXPERTS = 32
TOP_K = 4
SWIGLU_ALPHA = 1.702
SWIGLU_LIMIT = 7.0
VMEM_LIMIT = 56 * 1024 * 1024


def _params(sem, vmem=VMEM_LIMIT):
    return pltpu.CompilerParams(dimension_semantics=sem, vmem_limit_bytes=vmem)


def _rms(x, g):
    return x * lax.rsqrt(jnp.mean(x * x, axis=-1, keepdims=True) + RMS_EPS) * g


def _log_sigmoid(x):
    return jnp.minimum(x, 0.0) - jnp.log1p(jnp.exp(-jnp.abs(x)))


def _split3(x):
    hi = x.astype(BF16)
    r1 = x - hi.astype(F32)
    mid = r1.astype(BF16)
    lo = (r1 - mid.astype(F32)).astype(BF16)
    return hi, mid, lo


def _dot(a, b):
    return jnp.dot(a, b, preferred_element_type=F32)


def _dot_nt(a, b):
    return lax.dot_general(a, b, (((1,), (1,)), ((), ())), preferred_element_type=F32)


def _dot3(x, m):
    hi, mid, lo = _split3(x)
    return _dot(hi, m) + _dot(mid, m) + _dot(lo, m)


def _s5_params_kernel(lr_ref, li_ref, ls_ref, brt_ref, bit_ref, ar_ref, ai_ref, bbr_ref, bbi_ref):
    lr, li = lr_ref[...], li_ref[...]
    step = jnp.exp(ls_ref[...])
    mag = jnp.exp(lr * step)
    a_re, a_im = mag * jnp.cos(li * step), mag * jnp.sin(li * step)
    nr, ni = a_re - 1.0, a_im
    den = lr * lr + li * li
    coef_re = (nr * lr + ni * li) / den
    coef_im = (ni * lr - nr * li) / den
    ar_ref[...] = a_re
    ai_ref[...] = a_im
    br, bi = brt_ref[...], bit_ref[...]
    bbr_ref[...] = coef_re * br - coef_im * bi
    bbi_ref[...] = coef_re * bi + coef_im * br


def _s5_params(lam_re, lam_im, log_step, b_re_t, b_im_t):
    g, p = lam_re.shape
    c = b_re_t.shape[1]
    return pl.pallas_call(
        _s5_params_kernel,
        out_shape=(jax.ShapeDtypeStruct((g, 1, p), F32), jax.ShapeDtypeStruct((g, 1, p), F32),
                   jax.ShapeDtypeStruct((g, c, p), F32), jax.ShapeDtypeStruct((g, c, p), F32)),
        name="s5_params",
    )(lam_re.reshape(g, 1, p), lam_im.reshape(g, 1, p), log_step.reshape(g, 1, 1), b_re_t, b_im_t)


def _inproj_core(x, g_ref, w_ref, aw):
    h = _rms(x, g_ref[...]).astype(BF16)
    z = _dot(h, w_ref[...])
    return h, z


def _inproj_prompt_kernel(x_ref, g_ref, w_ref, wft_ref, bf_ref, utri_ref,
                          q_ref, k_ref, v_ref, kb_ref, vb_ref, u_ref, lft_ref, ct_ref, carry_ref, *, aw, scale):
    tm = x_ref.shape[1]
    h, z = _inproj_core(x_ref[0], g_ref, w_ref, aw)
    q_ref[0] = (z[:, :aw] * scale).astype(BF16)
    k = z[:, aw:2 * aw]
    v = z[:, 2 * aw:3 * aw]
    k_ref[0] = k
    v_ref[0] = v
    kb_ref[0] = k.astype(BF16)
    vb_ref[0] = v.astype(BF16)
    u_ref[0] = z[:, 3 * aw:4 * aw]
    ft = _dot_nt(wft_ref[...], h) + bf_ref[...]
    lft = _log_sigmoid(ft)
    n_heads = lft_ref.shape[1]
    lft_ref[0] = lft[:n_heads]

    @pl.when(pl.program_id(1) == 0)
    def _():
        carry_ref[...] = jnp.zeros_like(carry_ref)

    cs = _dot3(lft, utri_ref[...]) + carry_ref[:, :1]
    carry_ref[...] = jnp.broadcast_to(cs[:, tm - 1:tm], carry_ref.shape)
    ct_ref[0] = cs[:n_heads]


def _inproj_prompt(x, g_mix, w_all, wft, bf_col, *, aw, n_heads, tm):
    b, s, d = x.shape
    tm = min(tm, s)
    assert s % tm == 0
    ii = lax.broadcasted_iota(I32, (tm, tm), 0)
    jj = lax.broadcasted_iota(I32, (tm, tm), 1)
    utri = (ii <= jj).astype(BF16)
    row = lambda bi, si: (bi, si, 0)
    col = lambda bi, si: (bi, 0, si)
    const = lambda bi, si: (0, 0)
    wide = pl.BlockSpec((1, tm, aw), row)
    tall = pl.BlockSpec((1, n_heads, tm), col)
    return pl.pallas_call(
        functools.partial(_inproj_prompt_kernel, aw=aw, scale=HEAD_DIM ** -0.5),
        grid=(b, s // tm),
        in_specs=[pl.BlockSpec((1, tm, d), row), pl.BlockSpec((1, d), const), pl.BlockSpec(w_all.shape, const),
                  pl.BlockSpec(wft.shape, const), pl.BlockSpec(bf_col.shape, const), pl.BlockSpec((tm, tm), const)],
        out_specs=[wide, wide, wide, wide, wide, wide, tall, tall],
        out_shape=[jax.ShapeDtypeStruct((b, s, aw), BF16), jax.ShapeDtypeStruct((b, s, aw), F32),
                   jax.ShapeDtypeStruct((b, s, aw), F32), jax.ShapeDtypeStruct((b, s, aw), BF16),
                   jax.ShapeDtypeStruct((b, s, aw), BF16), jax.ShapeDtypeStruct((b, s, aw), F32),
                   jax.ShapeDtypeStruct((b, n_heads, s), F32), jax.ShapeDtypeStruct((b, n_heads, s), F32)],
        scratch_shapes=[pltpu.VMEM((2 * SUBLANES, LANES), F32)],
        compiler_params=_params(("arbitrary", "arbitrary")),
        name="inproj_prompt",
    )(x, g_mix, w_all, wft, bf_col, utri)


def _inproj_sample_kernel(x_ref, g_ref, w_ref, bf_ref, q_ref, k_ref, v_ref, u_ref, lf_ref, *, aw, scale):
    _, z = _inproj_core(x_ref[...], g_ref, w_ref, aw)
    q_ref[...] = (z[:, :aw] * scale).astype(BF16)
    k_ref[...] = z[:, aw:2 * aw]
    v_ref[...] = z[:, 2 * aw:3 * aw]
    u_ref[...] = z[:, 3 * aw:4 * aw]
    lf_ref[...] = _log_sigmoid(z[:, 4 * aw:] + bf_ref[...])


def _inproj_sample(x, g_mix, w_all, bf_row, *, aw):
    t, d = x.shape
    outs = [jax.ShapeDtypeStruct((t, aw), BF16)] + [jax.ShapeDtypeStruct((t, aw), F32)] * 3
    outs.append(jax.ShapeDtypeStruct((t, LANES), F32))
    return pl.pallas_call(
        functools.partial(_inproj_sample_kernel, aw=aw, scale=HEAD_DIM ** -0.5),
        out_shape=outs, compiler_params=_params(None), name="inproj_sample",
    )(x, g_mix, w_all, bf_row)


def _fox_prompt_kernel(q_ref, k_ref, v_ref, ct_ref, o_ref, *, tq):
    qi = pl.program_id(2)
    pair = pl.program_id(1)
    q = q_ref[0]
    lane = lax.broadcasted_iota(I32, (tq, LANES), 1)
    row = lax.broadcasted_iota(I32, (tq, tq), 0)
    colm = lax.broadcasted_iota(I32, (tq, tq), 1)
    outs = []
    for hh in range(HEADS_PER_LANE_TILE):
        in_head = (lane >= hh * HEAD_DIM) & (lane < (hh + 1) * HEAD_DIM)
        qh = jnp.where(in_head, q, jnp.zeros_like(q))
        head = pair * HEADS_PER_LANE_TILE + hh

        def tile(j, carry, masked):
            m, l, acc = carry
            start = pl.multiple_of(j * tq, tq)
            kj = k_ref[0, pl.ds(start, tq), :]
            vj = v_ref[0, pl.ds(start, tq), :]
            s = _dot_nt(qh, kj) - ct_ref[0, head, pl.ds(j, 1), :]
            if masked:
                s = jnp.where(colm <= row, s, -jnp.inf)
            m_new = jnp.maximum(m, jnp.max(s, axis=-1, keepdims=True))
            alpha = jnp.exp(m - m_new)
            p = jnp.exp(s - m_new)
            l = alpha * l + jnp.sum(p, axis=-1, keepdims=True)
            acc = alpha * acc + _dot(p.astype(BF16), vj)
            return m_new, l, acc

        init = (jnp.full((tq, 1), -jnp.inf, F32), jnp.zeros((tq, 1), F32), jnp.zeros((tq, LANES), F32))
        carry = lax.fori_loop(0, qi, lambda j, c: tile(j, c, False), init)
        _, l, acc = tile(qi, carry, True)
        outs.append(acc / l)
    o_ref[0] = jnp.where(lane < HEAD_DIM, outs[0], outs[1])


def _fox_prompt(q, kb, vb, ct, *, tq):
    b, s, aw = q.shape
    tq = min(tq, s)
    assert s % tq == 0
    n_pairs = aw // LANES
    ct = ct.reshape(b, ct.shape[1], s // tq, tq)
    return pl.pallas_call(
        functools.partial(_fox_prompt_kernel, tq=tq),
        grid=(b, n_pairs, s // tq),
        in_specs=[pl.BlockSpec((1, tq, LANES), lambda bi, p, qi: (bi, qi, p)),
                  pl.BlockSpec((1, s, LANES), lambda bi, p, qi: (bi, 0, p)),
                  pl.BlockSpec((1, s, LANES), lambda bi, p, qi: (bi, 0, p)),
                  pl.BlockSpec((1,) + ct.shape[1:], lambda bi, p, qi: (bi, 0, 0, 0))],
        out_specs=pl.BlockSpec((1, tq, LANES), lambda bi, p, qi: (bi, qi, p)),
        out_shape=jax.ShapeDtypeStruct((b, s, aw), F32),
        compiler_params=_params(("arbitrary", "arbitrary", "arbitrary")),
        name="fox_prompt",
    )(q, kb, vb, ct)


def _fox_decode_kernel(pt_ref, q_ref, kn_ref, vn_ref, lfn_ref, ustrict_ref, *rest, n_group, n_new):
    kt_refs = rest[:n_group]
    vt_refs = rest[n_group:2 * n_group]
    lf_refs = rest[2 * n_group:3 * n_group]
    o_ref, m_ref, l_ref, acc_ref, tail_ref = rest[3 * n_group:]
    j = pl.program_id(1)
    n_heads = lfn_ref.shape[1]
    width = q_ref.shape[2]
    rows = n_new * n_heads

    lane_head = lax.broadcasted_iota(I32, (n_heads, width), 1) // HEAD_DIM
    sub = lax.broadcasted_iota(I32, (n_heads, width), 0)
    hmask = lane_head == sub
    q = q_ref[0].astype(F32)
    qexp = jnp.concatenate([jnp.where(hmask, jnp.broadcast_to(q[t:t + 1], (n_heads, width)), 0.0)
                            for t in range(n_new)], axis=0)
    trow = lax.broadcasted_iota(I32, (rows, 1), 0) // n_heads

    @pl.when(j == 0)
    def _():
        lfn = lfn_ref[0]
        kn, vn = kn_ref[0], vn_ref[0]
        run = jnp.zeros((n_heads, 1), F32)
        logits = []
        for sp in range(n_new):
            run = run + lfn[:, sp:sp + 1]
            bias = jnp.concatenate([run] * n_new, axis=0)
            sc = jnp.sum(qexp * kn[sp:sp + 1], axis=-1, keepdims=True) - bias
            logits.append(jnp.where(trow >= sp, sc, -jnp.inf))
        m = logits[0]
        for sc in logits[1:]:
            m = jnp.maximum(m, sc)
        l = jnp.zeros((rows, 1), F32)
        acc = jnp.zeros((rows, width), F32)
        for sp in range(n_new):
            p = jnp.exp(logits[sp] - m)
            l = l + p
            acc = acc + p * vn[sp:sp + 1]
        m_ref[...] = m
        l_ref[...] = l
        acc_ref[...] = acc
        tail_ref[...] = jnp.zeros_like(tail_ref)

    qb = qexp.astype(BF16)
    for g in range(n_group):
        kt = kt_refs[g][0].reshape(width, -1).astype(BF16)
        vt = vt_refs[g][0].reshape(width, -1).astype(BF16)
        lf = lf_refs[g][0]
        tail = tail_ref[:, :1]
        lf_pad = jnp.concatenate([lf, jnp.zeros_like(lf)], axis=0)
        suffix = _dot3(lf_pad, ustrict_ref[...])[:n_heads] + tail
        tail_ref[...] = jnp.broadcast_to(tail + jnp.sum(lf, axis=-1, keepdims=True), tail_ref.shape)
        s = _dot(qb, kt) + jnp.concatenate([suffix] * n_new, axis=0)
        m_old = m_ref[...]
        m_new = jnp.maximum(m_old, jnp.max(s, axis=-1, keepdims=True))
        alpha = jnp.exp(m_old - m_new)
        p = jnp.exp(s - m_new)
        l_ref[...] = alpha * l_ref[...] + jnp.sum(p, axis=-1, keepdims=True)
        acc_ref[...] = alpha * acc_ref[...] + _dot_nt(p.astype(BF16), vt)
        m_ref[...] = m_new

    @pl.when(j == pl.num_programs(1) - 1)
    def _():
        o = acc_ref[...] / l_ref[...]
        outs = [jnp.sum(jnp.where(hmask, o[t * n_heads:(t + 1) * n_heads], 0.0), axis=0, keepdims=True)
                for t in range(n_new)]
        o_ref[0] = jnp.concatenate(outs, axis=0)


def _fox_decode(page_table, q, k_new, v_new, lfn_t, kt_cache, vt_cache, lft_cache, *, n_group):
    bd, n_new, width = q.shape
    n_pages = page_table.shape[1]
    n_heads, page = lft_cache.shape[1:]
    n_group = min(n_group, n_pages)
    assert n_pages % n_group == 0
    n_steps = n_pages // n_group
    ii = lax.broadcasted_iota(I32, (page, page), 0)
    jj = lax.broadcasted_iota(I32, (page, page), 1)
    ustrict = (ii > jj).astype(BF16)

    def page_map(g, ndim):
        def index_map(b, j, pt):
            return (pt[b, n_pages - 1 - (j * n_group + g)],) + (0,) * ndim
        return index_map

    per_batch = lambda b, j, pt: (b, 0, 0)
    in_specs = [pl.BlockSpec((1, n_new, width), per_batch), pl.BlockSpec((1, n_new, width), per_batch),
                pl.BlockSpec((1, n_new, width), per_batch), pl.BlockSpec((1, n_heads, n_new), per_batch),
                pl.BlockSpec((page, page), lambda b, j, pt: (0, 0))]
    in_specs += [pl.BlockSpec((1,) + kt_cache.shape[1:], page_map(g, 3)) for g in range(n_group)]
    in_specs += [pl.BlockSpec((1,) + vt_cache.shape[1:], page_map(g, 3)) for g in range(n_group)]
    in_specs += [pl.BlockSpec((1, n_heads, page), page_map(g, 2)) for g in range(n_group)]
    rows = n_new * n_heads
    grid_spec = pltpu.PrefetchScalarGridSpec(
        num_scalar_prefetch=1, grid=(bd, n_steps), in_specs=in_specs,
        out_specs=pl.BlockSpec((1, n_new, width), per_batch),
        scratch_shapes=[pltpu.VMEM((rows, 1), F32), pltpu.VMEM((rows, 1), F32), pltpu.VMEM((rows, width), F32),
                        pltpu.VMEM((n_heads, LANES), F32)])
    return pl.pallas_call(
        functools.partial(_fox_decode_kernel, n_group=n_group, n_new=n_new),
        grid_spec=grid_spec, out_shape=jax.ShapeDtypeStruct((bd, n_new, width), F32),
        compiler_params=_params(("arbitrary", "arbitrary")), name="fox_decode",
    )(page_table, q, k_new, v_new, lfn_t, ustrict,
      *([kt_cache] * n_group), *([vt_cache] * n_group), *([lft_cache] * n_group))


def _ssm_kernel(u_ref, h0r_ref, h0i_ref, ar_ref, ai_ref, bd_ref, cdr_ref, cdi_ref, d_ref, wglu_ref, bglu_ref,
                gout_ref, o_ref, hr_ref, hi_ref, bu_ref, h_ref, *, n_batch, n_time):
    i = pl.program_id(0)
    half = ar_ref.shape[1]
    blk = half // SSM_LANE_BLOCKS
    rows = n_batch * n_time

    @pl.when(i == 0)
    def _():
        h_ref[:, :half] = h0r_ref[...]
        h_ref[:, half:] = h0i_ref[...]

    u = u_ref[...]
    ub = u.astype(BF16)
    for jb in range(SSM_LANE_BLOCKS):
        bu = _dot(ub[:, jb * LANES:(jb + 1) * LANES], bd_ref[jb])
        bu_ref[:, jb * blk:(jb + 1) * blk] = bu[:, :blk]
        bu_ref[:, half + jb * blk:half + (jb + 1) * blk] = bu[:, blk:]

    chunk = blk
    for bg in range(n_batch // SUBLANES):
        for c in range(half // chunk):
            lo_r, lo_i = c * chunk, half + c * chunk
            a_r = jnp.broadcast_to(ar_ref[:, lo_r:lo_r + chunk], (SUBLANES, chunk))
            a_i = jnp.broadcast_to(ai_ref[:, lo_r:lo_r + chunk], (SUBLANES, chunk))
            h_r0 = h_ref[bg * SUBLANES:(bg + 1) * SUBLANES, lo_r:lo_r + chunk]
            h_i0 = h_ref[bg * SUBLANES:(bg + 1) * SUBLANES, lo_i:lo_i + chunk]

            def step(t, carry, lo_r=lo_r, lo_i=lo_i, a_r=a_r, a_i=a_i, bg=bg):
                h_r, h_i = carry
                r0 = pl.multiple_of(t * n_batch + bg * SUBLANES, SUBLANES)
                n_r = a_r * h_r - a_i * h_i + bu_ref[pl.ds(r0, SUBLANES), lo_r:lo_r + chunk]
                n_i = a_r * h_i + a_i * h_r + bu_ref[pl.ds(r0, SUBLANES), lo_i:lo_i + chunk]
                bu_ref[pl.ds(r0, SUBLANES), lo_r:lo_r + chunk] = n_r
                bu_ref[pl.ds(r0, SUBLANES), lo_i:lo_i + chunk] = n_i
                return n_r, n_i

            h_r, h_i = lax.fori_loop(0, n_time, step, (h_r0, h_i0), unroll=min(4, n_time))
            h_ref[bg * SUBLANES:(bg + 1) * SUBLANES, lo_r:lo_r + chunk] = h_r
            h_ref[bg * SUBLANES:(bg + 1) * SUBLANES, lo_i:lo_i + chunk] = h_i

    ys = []
    for jb in range(SSM_LANE_BLOCKS):
        hre = bu_ref[:, jb * blk:(jb + 1) * blk].astype(BF16)
        him = bu_ref[:, half + jb * blk:half + (jb + 1) * blk].astype(BF16)
        ys.append(_dot(hre, cdr_ref[jb]) - _dot(him, cdi_ref[jb]))
    y = jnp.concatenate(ys, axis=-1) + d_ref[...] * u
    y = jax.nn.gelu(y, approximate=True)
    y = y * jax.nn.sigmoid(_dot(y.astype(BF16), wglu_ref[...]) + bglu_ref[...])
    o_ref[...] = _rms(y, gout_ref[...]).astype(BF16)

    @pl.when(i == pl.num_programs(0) - 1)
    def _():
        hr_ref[...] = h_ref[:, :half]
        hi_ref[...] = h_ref[:, half:]


def _ssm(u_tb, h0r, h0i, abar_re, abar_im, bd, cdr, cdi, d_row, w_glu, b_glu, g_out, *, n_batch, n_time):
    total, width = u_tb.shape
    half = abar_re.shape[1]
    rows = n_batch * n_time
    assert total % rows == 0 and n_batch % SUBLANES == 0
    const2 = lambda i: (0, 0)
    const3 = lambda i: (0, 0, 0)
    full = lambda a: pl.BlockSpec(a.shape, const2 if a.ndim == 2 else const3)
    return pl.pallas_call(
        functools.partial(_ssm_kernel, n_batch=n_batch, n_time=n_time),
        grid=(total // rows,),
        in_specs=[pl.BlockSpec((rows, width), lambda i: (i, 0))] + [full(a) for a in (
            h0r, h0i, abar_re, abar_im, bd, cdr, cdi, d_row, w_glu, b_glu, g_out)],
        out_specs=[pl.BlockSpec((rows, width), lambda i: (i, 0)), pl.BlockSpec((n_batch, half), const2),
                   pl.BlockSpec((n_batch, half), const2)],
        out_shape=[jax.ShapeDtypeStruct((total, width), BF16), jax.ShapeDtypeStruct((n_batch, half), F32),
                   jax.ShapeDtypeStruct((n_batch, half), F32)],
        scratch_shapes=[pltpu.VMEM((rows, 2 * half), F32), pltpu.VMEM((n_batch, 2 * half), F32)],
        compiler_params=_params(("arbitrary",)), name="ssm",
    )(u_tb, h0r, h0i, abar_re, abar_im, bd, cdr, cdi, d_row, w_glu, b_glu, g_out)


def _outproj_kernel(x_ref, a_ref, s_ref, ga_ref, woa_ref, wos_ref, gffn_ref, wr_ref, br_ref, lstrict_ref,
                    x1_ref, hm_ref, idx_ref, gate_ref, rank_ref, cnt_ref, cnt_sc):
    i = pl.program_id(0)
    tm = x_ref.shape[0]

    @pl.when(i == 0)
    def _():
        cnt_sc[...] = jnp.zeros_like(cnt_sc)

    an = _rms(a_ref[...], ga_ref[...]).astype(BF16)
    x1 = x_ref[...] + _dot(an, woa_ref[...]) + _dot(s_ref[...], wos_ref[...])
    x1_ref[...] = x1
    hm = _rms(x1, gffn_ref[...])
    hm_ref[...] = hm
    logits = _dot(hm.astype(BF16), wr_ref[...]) + br_ref[...]
    lane_i = lax.broadcasted_iota(I32, (tm, LANES), 1)
    lane = lane_i.astype(F32)
    work = jnp.where(lane_i < N_EXPERTS, logits, -jnp.inf)
    vals, idxs = [], []
    for _ in range(TOP_K):
        mx = jnp.max(work, axis=-1, keepdims=True)
        ix = jnp.min(jnp.where(work == mx, lane, float(LANES)), axis=-1, keepdims=True)
        vals.append(mx)
        idxs.append(ix)
        work = jnp.where(lane == ix, -jnp.inf, work)
    exps = [jnp.exp(v - vals[0]) for v in vals]
    denom = exps[0]
    for e in exps[1:]:
        denom = denom + e
    onehot = jnp.zeros((tm, LANES), F32)
    for ix in idxs:
        onehot = onehot + (lane == ix).astype(F32)
    before = _dot(lstrict_ref[...], onehot.astype(BF16)) + cnt_sc[:1, :]
    idx_out = jnp.zeros((tm, LANES), I32)
    gate_out = jnp.zeros((tm, LANES), F32)
    rank_out = jnp.zeros((tm, LANES), I32)
    for k in range(TOP_K):
        rk = jnp.sum(jnp.where(lane == idxs[k], before, 0.0), axis=-1, keepdims=True)
        idx_out = jnp.where(lane_i == k, idxs[k].astype(I32), idx_out)
        gate_out = jnp.where(lane_i == k, exps[k] / denom, gate_out)
        rank_out = jnp.where(lane_i == k, rk.astype(I32), rank_out)
    idx_ref[...] = idx_out
    gate_ref[...] = gate_out
    rank_ref[...] = rank_out
    cnt = cnt_sc[:1, :] + jnp.sum(onehot, axis=0, keepdims=True)
    cnt_sc[...] = jnp.broadcast_to(cnt, cnt_sc.shape)
    cnt_ref[...] = jnp.broadcast_to(cnt, cnt_ref.shape)


def _outproj(x, attn, ssm_n, g_attn, wo_a, wo_s, g_ffn, wr_pad, br_pad, *, tm):
    t, d = x.shape
    tm = min(tm, t)
    assert t % tm == 0
    ii = lax.broadcasted_iota(I32, (tm, tm), 0)
    jj = lax.broadcasted_iota(I32, (tm, tm), 1)
    lstrict = (jj < ii).astype(BF16)
    rowsp = lambda w: pl.BlockSpec((tm, w), lambda i: (i, 0))
    full = lambda a: pl.BlockSpec(a.shape, lambda i: (0, 0))
    aw = attn.shape[1]
    return pl.pallas_call(
        _outproj_kernel, grid=(t // tm,),
        in_specs=[rowsp(d), rowsp(aw), rowsp(ssm_n.shape[1])] + [full(a) for a in (
            g_attn, wo_a, wo_s, g_ffn, wr_pad, br_pad, lstrict)],
        out_specs=[rowsp(d), rowsp(d), rowsp(LANES), rowsp(LANES), rowsp(LANES),
                   pl.BlockSpec((SUBLANES, LANES), lambda i: (0, 0))],
        out_shape=[jax.ShapeDtypeStruct((t, d), F32), jax.ShapeDtypeStruct((t, d), F32),
                   jax.ShapeDtypeStruct((t, LANES), I32), jax.ShapeDtypeStruct((t, LANES), F32),
                   jax.ShapeDtypeStruct((t, LANES), I32), jax.ShapeDtypeStruct((SUBLANES, LANES), F32)],
        scratch_shapes=[pltpu.VMEM((SUBLANES, LANES), F32)],
        compiler_params=_params(("arbitrary",)), name="outproj_router",
    )(x, attn, ssm_n, g_attn, wo_a, wo_s, g_ffn, wr_pad, br_pad, lstrict)


def _experts_kernel(ie_ref, ib_ref, lo_ref, hi_ref, fl_ref, ni_ref, tok_ref, tokn_ref, dst_ref, hm_ref,
                    wup_ref, bup_ref, wdn_ref, bdn_ref, y_ref, xbuf, obuf, sem_in, sem_out, *, bm, d_ff, n_blocks):
    w = pl.program_id(0)
    n_items = ni_ref[0]
    active = w < n_items
    blk = ib_ref[w]
    slot = blk % 2
    first = (fl_ref[w] & 1) == 1
    last = (fl_ref[w] & 2) == 2

    def gather(rows_ref, sl):
        def body(r, c):
            pltpu.make_async_copy(hm_ref.at[pl.ds(rows_ref[0, 0, r], 1)], xbuf.at[sl, pl.ds(r, 1)],
                                  sem_in.at[sl]).start()
            return c
        lax.fori_loop(0, bm, body, 0)

    def wait_rows(buf, sem):
        pltpu.make_async_copy(hm_ref.at[pl.ds(0, bm)], buf, sem).wait()

    @pl.when(w == 0)
    def _():
        gather(tok_ref, 0)

    @pl.when(active & first & (blk + 1 < n_blocks))
    def _():
        gather(tokn_ref, 1 - slot)

    @pl.when(active)
    def _():
        @pl.when(first)
        def _():
            wait_rows(xbuf.at[slot], sem_in.at[slot])

        x = xbuf[slot].astype(BF16)
        up = _dot(x, wup_ref[0]) + bup_ref[0]
        gate = jnp.minimum(up[:, :d_ff], SWIGLU_LIMIT)
        lin = jnp.clip(up[:, d_ff:], -SWIGLU_LIMIT, SWIGLU_LIMIT)
        act = gate * jax.nn.sigmoid(SWIGLU_ALPHA * gate) * (lin + 1.0)
        out = _dot(act.astype(BF16), wdn_ref[0]) + bdn_ref[0]

        @pl.when(first)
        def _():
            @pl.when(blk > 0)
            def _():
                wait_rows(obuf, sem_out.at[0])
            obuf[...] = out

        @pl.when(jnp.logical_not(first))
        def _():
            rowi = lax.broadcasted_iota(I32, (bm, 1), 0)
            mine = (rowi >= lo_ref[w]) & (rowi < hi_ref[w])
            obuf[...] = jnp.where(mine, out, obuf[...])

        @pl.when(last)
        def _():
            def body(r, c):
                pltpu.make_async_copy(obuf.at[pl.ds(r, 1)], y_ref.at[pl.ds(dst_ref[0, 0, r], 1)],
                                      sem_out.at[0]).start()
                return c
            lax.fori_loop(0, bm, body, 0)

    @pl.when(w == n_items - 1)
    def _():
        wait_rows(obuf, sem_out.at[0])


def _experts(route, hm, w_up, b_up, w_down, b_down, *, bm):
    item_expert, item_block, item_lo, item_hi, item_flags, n_items, row_tok, row_dst = route
    t, d = hm.shape
    n_blocks = row_tok.shape[0]
    n_work = item_expert.shape[0]
    d_ff = w_down.shape[1]
    smem_blk = lambda f: pl.BlockSpec((1, 1, bm), f, memory_space=pltpu.SMEM)
    cur = lambda w, ie, ib, lo, hi, fl, ni: (ib[w], 0, 0)
    nxt = lambda w, ie, ib, lo, hi, fl, ni: (jnp.minimum(ib[w] + 1, n_blocks - 1), 0, 0)
    by_expert = lambda w, ie, ib, lo, hi, fl, ni: (ie[w], 0, 0)
    grid_spec = pltpu.PrefetchScalarGridSpec(
        num_scalar_prefetch=6, grid=(n_work,),
        in_specs=[smem_blk(cur), smem_blk(nxt), smem_blk(cur), pl.BlockSpec(memory_space=pl.ANY),
                  pl.BlockSpec((1,) + w_up.shape[1:], by_expert), pl.BlockSpec((1, 1, b_up.shape[2]), by_expert),
                  pl.BlockSpec((1,) + w_down.shape[1:], by_expert), pl.BlockSpec((1, 1, d), by_expert)],
        out_specs=pl.BlockSpec(memory_space=pl.ANY),
        scratch_shapes=[pltpu.VMEM((2, bm, d), F32), pltpu.VMEM((bm, d), F32),
                        pltpu.SemaphoreType.DMA((2,)), pltpu.SemaphoreType.DMA((1,))])
    return pl.pallas_call(
        functools.partial(_experts_kernel, bm=bm, d_ff=d_ff, n_blocks=n_blocks),
        grid_spec=grid_spec, out_shape=jax.ShapeDtypeStruct((n_blocks * bm, d), F32),
        compiler_params=_params(("arbitrary",)), name="experts",
    )(item_expert, item_block, item_lo, item_hi, item_flags, n_items, row_tok, row_tok, row_dst,
      hm, w_up, b_up, w_down, b_down)


def _route(idx, rank, counts, *, bm):
    t = idx.shape[0]
    n_rows = t * TOP_K
    assert n_rows % bm == 0
    n_blocks = n_rows // bm
    n_work = n_blocks + N_EXPERTS - 1
    cnt = counts.astype(I32)
    end = jnp.cumsum(cnt)
    start = end - cnt
    slot = (start[idx] + rank).reshape(-1)
    tok = jnp.repeat(jnp.arange(t, dtype=I32), TOP_K)
    dst = jnp.tile(jnp.arange(TOP_K, dtype=I32), t) * t + tok
    row_tok = jnp.zeros((n_rows,), I32).at[slot].set(tok)
    row_dst = jnp.zeros((n_rows,), I32).at[slot].set(dst)
    first_blk = start // bm
    n_touch = jnp.where(cnt > 0, (end - 1) // bm - first_blk + 1, 0)
    wend = jnp.cumsum(n_touch)
    wstart = wend - n_touch
    n_items = wend[-1]
    w = jnp.minimum(jnp.arange(n_work, dtype=I32), n_items - 1)
    e = jnp.minimum(jnp.searchsorted(wend, w, side="right"), N_EXPERTS - 1).astype(I32)
    blk = first_blk[e] + (w - wstart[e])
    lo = jnp.maximum(start[e] - blk * bm, 0)
    hi = jnp.minimum(end[e] - blk * bm, bm)
    flags = (start[e] <= blk * bm).astype(I32) + 2 * (end[e] >= (blk + 1) * bm).astype(I32)
    return (e, blk.astype(I32), lo.astype(I32), hi.astype(I32), flags, n_items.reshape(1).astype(I32),
            row_tok.reshape(n_blocks, 1, bm), row_dst.reshape(n_blocks, 1, bm))


def _tail_kernel(x1_ref, y0_ref, y1_ref, y2_ref, y3_ref, gate_ref, p_ref, gple_ref, wg_ref, wp_ref, gfin_ref, o_ref):
    gates = gate_ref[...]
    x2 = x1_ref[...]
    for k, y_ref in enumerate((y0_ref, y1_ref, y2_ref, y3_ref)):
        x2 = x2 + gates[:, k:k + 1] * y_ref[...]
    g = jax.nn.sigmoid(_dot(_rms(x2, gple_ref[...]).astype(BF16), wg_ref[...]))
    x3 = x2 + _dot(p_ref[...].astype(BF16), wp_ref[...]) * g
    o_ref[...] = _rms(x3, gfin_ref[...])


def _tail(x1, y4, gates, p, g_ple, w_gate, w_proj, g_final, *, tm):
    t, d = x1.shape
    tm = min(tm, t)
    assert t % tm == 0
    nt = t // tm
    rowsp = lambda w: pl.BlockSpec((tm, w), lambda i: (i, 0))
    full = lambda a: pl.BlockSpec(a.shape, lambda i: (0, 0))
    yspec = lambda k: pl.BlockSpec((tm, d), lambda i, k=k: (k * nt + i, 0))
    return pl.pallas_call(
        _tail_kernel, grid=(nt,),
        in_specs=[rowsp(d)] + [yspec(k) for k in range(TOP_K)] + [rowsp(LANES), rowsp(p.shape[1])]
        + [full(a) for a in (g_ple, w_gate, w_proj, g_final)],
        out_specs=rowsp(d), out_shape=jax.ShapeDtypeStruct((t, d), F32),
        compiler_params=_params(("arbitrary",)), name="tail",
    )(x1, y4, y4, y4, y4, gates, p, g_ple, w_gate, w_proj, g_final)


def _block_diag(a, n):
    j, _, r, c = a.shape
    eye = jnp.eye(n, dtype=a.dtype)
    return jnp.einsum("ab,jarc->jarbc", eye, a).reshape(j, n * r, n * c)


def _mixer_tail(x_tok, attn, ssm_n, p_tok, lw, *, tm, bm):
    t, d = x_tok.shape
    x1, hm, idx, gates, rank, counts = _outproj(
        x_tok, attn, ssm_n, lw["g_attn"], lw["wo_a"], lw["wo_s"], lw["g_ffn"], lw["wr_pad"], lw["br_pad"], tm=tm)
    route = _route(idx[:, :TOP_K], rank[:, :TOP_K], counts[0, :N_EXPERTS], bm=bm)
    y4 = _experts(route, hm, lw["w_up"], lw["b_up"], lw["w_down"], lw["b_down"], bm=bm)
    return _tail(x1, y4, gates, p_tok, lw["g_ple"], lw["w_ple_gate"], lw["w_ple_proj"], lw["g_final"], tm=tm)


def kernel(x_prompt, x_sample, p_prompt, p_sample, cache_k, cache_v, cache_logf, state_ssm_re, state_ssm_im,
           page_table, g_mix, w_in, b_fgate, lam_re, lam_im, log_step, b_ssm_re, b_ssm_im, c_ssm_re, c_ssm_im,
           d_ssm, w_glu, b_glu, g_attn_out, g_ssm_out, w_out, g_ffn, w_router, b_router, w_up, b_up, w_down,
           b_down, g_ple, w_ple_gate, w_ple_proj, g_final):
    depth = w_in.shape[0]
    assert depth == 1
    bp, sp, d = x_prompt.shape
    bd, sd, _ = x_sample.shape
    n_heads = b_fgate.shape[1]
    aw = n_heads * HEAD_DIM
    n_groups, n_state = lam_re.shape[1:]
    sw = n_groups * SSM_GROUP
    half = n_groups * n_state
    assert sw == SSM_LANE_BLOCKS * LANES and w_in.shape[2] == 3 * aw + n_heads + sw

    wi = w_in[0]
    wq, wk, wv = wi[:, :aw], wi[:, aw:2 * aw], wi[:, 2 * aw:3 * aw]
    wf, wu = wi[:, 3 * aw:3 * aw + n_heads], wi[:, 3 * aw + n_heads:]
    w_all = jnp.concatenate([wq, wk, wv, wu, jnp.pad(wf, ((0, 0), (0, LANES - n_heads)))], axis=1).astype(BF16)
    wft = jnp.pad(wf.T, ((0, 2 * SUBLANES - n_heads), (0, 0))).astype(BF16)
    bf_col = jnp.pad(b_fgate[0][:, None], ((0, 2 * SUBLANES - n_heads), (0, 0)))
    bf_row = jnp.pad(b_fgate, ((0, 0), (0, LANES - n_heads)))

    abar_re, abar_im, bbr_t, bbi_t = _s5_params(
        lam_re[0], lam_im[0], log_step[0], jnp.transpose(b_ssm_re[0], (0, 2, 1)), jnp.transpose(b_ssm_im[0], (0, 2, 1)))
    gpb = n_groups // SSM_LANE_BLOCKS
    blk4 = lambda a: a.reshape((SSM_LANE_BLOCKS, gpb) + a.shape[1:])
    bd_mat = jnp.concatenate([_block_diag(blk4(bbr_t), gpb), _block_diag(blk4(bbi_t), gpb)], axis=-1).astype(BF16)
    cdr = _block_diag(blk4(jnp.transpose(c_ssm_re[0], (0, 2, 1))), gpb).astype(BF16)
    cdi = _block_diag(blk4(jnp.transpose(c_ssm_im[0], (0, 2, 1))), gpb).astype(BF16)
    ssm_w = (abar_re.reshape(1, half), abar_im.reshape(1, half), bd_mat, cdr, cdi, d_ssm[0].reshape(1, sw),
             w_glu[0].astype(BF16), b_glu, g_ssm_out)

    lw = dict(g_attn=g_attn_out, wo_a=w_out[0, :aw].astype(BF16), wo_s=w_out[0, aw:].astype(BF16), g_ffn=g_ffn,
              wr_pad=jnp.pad(w_router[0], ((0, 0), (0, LANES - N_EXPERTS))).astype(BF16),
              br_pad=jnp.pad(b_router, ((0, 0), (0, LANES - N_EXPERTS))),
              w_up=w_up[0].astype(BF16), b_up=b_up[0][:, None, :], w_down=w_down[0].astype(BF16),
              b_down=b_down[0][:, None, :], g_ple=g_ple, w_ple_gate=w_ple_gate[0].astype(BF16),
              w_ple_proj=w_ple_proj[0].astype(BF16), g_final=g_final.reshape(1, d))

    q, k, v, kb, vb, u, lft, ct = _inproj_prompt(x_prompt, g_mix, w_all, wft, bf_col, aw=aw, n_heads=n_heads, tm=512)
    attn = _fox_prompt(q, kb, vb, ct, tq=512)
    n_time = min(64, sp)
    zeros_h = jnp.zeros((bp, half), F32)
    ssm_tb, hr_p, hi_p = _ssm(jnp.transpose(u, (1, 0, 2)).reshape(sp * bp, sw), zeros_h, zeros_h, *ssm_w,
                              n_batch=bp, n_time=n_time)
    ssm_p = jnp.transpose(ssm_tb.reshape(sp, bp, sw), (1, 0, 2)).reshape(bp * sp, sw)
    y_p = _mixer_tail(x_prompt.reshape(bp * sp, d), attn.reshape(bp * sp, aw), ssm_p,
                      p_prompt[0].reshape(bp * sp, -1), lw, tm=512, bm=256)

    ts = bd * sd
    qs, ks, vs, us, lfs = _inproj_sample(x_sample.reshape(ts, d), g_mix, w_all, bf_row, aw=aw)
    lfs = lfs[:, :n_heads]
    kt_cache = jnp.transpose(cache_k[0], (0, 2, 3, 1))
    vt_cache = jnp.transpose(cache_v[0], (0, 2, 3, 1))
    lft_cache = jnp.transpose(cache_logf[0], (0, 2, 1))
    attn_s = _fox_decode(page_table, qs.reshape(bd, sd, aw), ks.reshape(bd, sd, aw), vs.reshape(bd, sd, aw),
                         jnp.transpose(lfs.reshape(bd, sd, n_heads), (0, 2, 1)), kt_cache, vt_cache, lft_cache,
                         n_group=4)
    ssm_s_tb, hr_s, hi_s = _ssm(jnp.transpose(us.reshape(bd, sd, sw), (1, 0, 2)).reshape(ts, sw),
                                state_ssm_re[0].reshape(bd, half), state_ssm_im[0].reshape(bd, half), *ssm_w,
                                n_batch=bd, n_time=sd)
    ssm_s = jnp.transpose(ssm_s_tb.reshape(sd, bd, sw), (1, 0, 2)).reshape(ts, sw)
    y_s = _mixer_tail(x_sample.reshape(ts, d), attn_s.reshape(ts, aw), ssm_s, p_sample[0].reshape(ts, -1), lw,
                      tm=128, bm=128)

    heads = lambda a, b_, s_: a.reshape(1, b_, s_, n_heads, HEAD_DIM)
    state = lambda a, b_: a.reshape(1, b_, n_groups, n_state)
    return (y_p.reshape(bp, sp, d), y_s.reshape(bd, sd, d),
            heads(k, bp, sp), heads(v, bp, sp), jnp.transpose(lft, (0, 2, 1))[None],
            state(hr_p, bp), state(hi_p, bp),
            heads(ks, bd, sd), heads(vs, bd, sd), lfs.reshape(1, bd, sd, n_heads),
            state(hr_s, bd), state(hi_s, bd))
```

```python
import functools

import jax
import jax.numpy as jnp
from jax import lax
from jax.experimental import pallas as pl
from jax.experimental.pallas import tpu as pltpu

F32 = jnp.float32
BF16 = jnp.bfloat16
I32 = jnp.int32

RMS_EPS = 1e-6
HEAD_DIM = 64
HEADS_PER_LANE_TILE = 2
LANES = 128
SUBLANES = 8
SSM_GROUP = 16
SSM_STATE = 64
SSM_LANE_BLOCKS = 4
N_EXPERTS = 32
TOP_K = 4
GRANULE = 8
SWIGLU_ALPHA = 1.702
SWIGLU_LIMIT = 7.0
VMEM_LIMIT = 56 * 1024 * 1024


def _params(sem, vmem=VMEM_LIMIT):
    return pltpu.CompilerParams(dimension_semantics=sem, vmem_limit_bytes=vmem)


def _rms(x, g):
    return x * lax.rsqrt(jnp.mean(x * x, axis=-1, keepdims=True) + RMS_EPS) * g


def _log_sigmoid(x):
    return jnp.minimum(x, 0.0) - jnp.log1p(jnp.exp(-jnp.abs(x)))


def _split3(x):
    hi = x.astype(BF16)
    r1 = x - hi.astype(F32)
    mid = r1.astype(BF16)
    lo = (r1 - mid.astype(F32)).astype(BF16)
    return hi, mid, lo


def _dot(a, b):
    return jnp.dot(a, b, preferred_element_type=F32)


def _dot_nt(a, b):
    return lax.dot_general(a, b, (((1,), (1,)), ((), ())), preferred_element_type=F32)


def _dot3(x, m):
    hi, mid, lo = _split3(x)
    return _dot(hi, m) + _dot(mid, m) + _dot(lo, m)


def _s5_params_kernel(lr_ref, li_ref, ls_ref, brt_ref, bit_ref, ar_ref, ai_ref, bbr_ref, bbi_ref):
    lr, li = lr_ref[...], li_ref[...]
    step = jnp.exp(ls_ref[...])
    mag = jnp.exp(lr * step)
    a_re, a_im = mag * jnp.cos(li * step), mag * jnp.sin(li * step)
    nr, ni = a_re - 1.0, a_im
    den = lr * lr + li * li
    coef_re = (nr * lr + ni * li) / den
    coef_im = (ni * lr - nr * li) / den
    ar_ref[...] = a_re
    ai_ref[...] = a_im
    br, bi = brt_ref[...], bit_ref[...]
    bbr_ref[...] = coef_re * br - coef_im * bi
    bbi_ref[...] = coef_re * bi + coef_im * br


def _s5_params(lam_re, lam_im, log_step, b_re_t, b_im_t):
    g, p = lam_re.shape
    c = b_re_t.shape[1]
    return pl.pallas_call(
        _s5_params_kernel,
        out_shape=(jax.ShapeDtypeStruct((g, 1, p), F32), jax.ShapeDtypeStruct((g, 1, p), F32),
                   jax.ShapeDtypeStruct((g, c, p), F32), jax.ShapeDtypeStruct((g, c, p), F32)),
        name="s5_params",
    )(lam_re.reshape(g, 1, p), lam_im.reshape(g, 1, p), log_step.reshape(g, 1, 1), b_re_t, b_im_t)


def _inproj_core(x, g_ref, w_ref, aw):
    h = _rms(x, g_ref[...]).astype(BF16)
    z = _dot(h, w_ref[...])
    return h, z


def _inproj_prompt_kernel(x_ref, g_ref, w_ref, wft_ref, bf_ref, utri_ref,
                          q_ref, k_ref, v_ref, kb_ref, vb_ref, u_ref, lft_ref, ct_ref, carry_ref, *, aw, scale):
    tm = x_ref.shape[1]
    h, z = _inproj_core(x_ref[0], g_ref, w_ref, aw)
    q_ref[0] = (z[:, :aw] * scale).astype(BF16)
    k = z[:, aw:2 * aw]
    v = z[:, 2 * aw:3 * aw]
    k_ref[0] = k
    v_ref[0] = v
    kb_ref[0] = k.astype(BF16)
    vb_ref[0] = v.astype(BF16)
    u_ref[0] = z[:, 3 * aw:4 * aw]
    ft = _dot_nt(wft_ref[...], h) + bf_ref[...]
    lft = _log_sigmoid(ft)
    n_heads = lft_ref.shape[1]
    lft_ref[0] = lft[:n_heads]

    @pl.when(pl.program_id(1) == 0)
    def _():
        carry_ref[...] = jnp.zeros_like(carry_ref)

    cs = _dot3(lft, utri_ref[...]) + carry_ref[:, :1]
    carry_ref[...] = jnp.broadcast_to(cs[:, tm - 1:tm], carry_ref.shape)
    ct_ref[0] = cs[:n_heads]


def _inproj_prompt(x, g_mix, w_all, wft, bf_col, *, aw, n_heads, tm):
    b, s, d = x.shape
    tm = min(tm, s)
    assert s % tm == 0
    ii = lax.broadcasted_iota(I32, (tm, tm), 0)
    jj = lax.broadcasted_iota(I32, (tm, tm), 1)
    utri = (ii <= jj).astype(BF16)
    row = lambda bi, si: (bi, si, 0)
    col = lambda bi, si: (bi, 0, si)
    const = lambda bi, si: (0, 0)
    wide = pl.BlockSpec((1, tm, aw), row)
    tall = pl.BlockSpec((1, n_heads, tm), col)
    return pl.pallas_call(
        functools.partial(_inproj_prompt_kernel, aw=aw, scale=HEAD_DIM ** -0.5),
        grid=(b, s // tm),
        in_specs=[pl.BlockSpec((1, tm, d), row), pl.BlockSpec((1, d), const), pl.BlockSpec(w_all.shape, const),
                  pl.BlockSpec(wft.shape, const), pl.BlockSpec(bf_col.shape, const), pl.BlockSpec((tm, tm), const)],
        out_specs=[wide, wide, wide, wide, wide, wide, tall, tall],
        out_shape=[jax.ShapeDtypeStruct((b, s, aw), BF16), jax.ShapeDtypeStruct((b, s, aw), F32),
                   jax.ShapeDtypeStruct((b, s, aw), F32), jax.ShapeDtypeStruct((b, s, aw), BF16),
                   jax.ShapeDtypeStruct((b, s, aw), BF16), jax.ShapeDtypeStruct((b, s, aw), F32),
                   jax.ShapeDtypeStruct((b, n_heads, s), F32), jax.ShapeDtypeStruct((b, n_heads, s), F32)],
        scratch_shapes=[pltpu.VMEM((2 * SUBLANES, LANES), F32)],
        compiler_params=_params(("arbitrary", "arbitrary")),
        name="inproj_prompt",
    )(x, g_mix, w_all, wft, bf_col, utri)


def _inproj_sample_kernel(x_ref, g_ref, w_ref, bf_ref, q_ref, k_ref, v_ref, u_ref, lf_ref, *, aw, scale):
    _, z = _inproj_core(x_ref[...], g_ref, w_ref, aw)
    q_ref[...] = (z[:, :aw] * scale).astype(BF16)
    k_ref[...] = z[:, aw:2 * aw]
    v_ref[...] = z[:, 2 * aw:3 * aw]
    u_ref[...] = z[:, 3 * aw:4 * aw]
    lf_ref[...] = _log_sigmoid(z[:, 4 * aw:] + bf_ref[...])


def _inproj_sample(x, g_mix, w_all, bf_row, *, aw):
    t, d = x.shape
    outs = [jax.ShapeDtypeStruct((t, aw), BF16)] + [jax.ShapeDtypeStruct((t, aw), F32)] * 3
    outs.append(jax.ShapeDtypeStruct((t, LANES), F32))
    return pl.pallas_call(
        functools.partial(_inproj_sample_kernel, aw=aw, scale=HEAD_DIM ** -0.5),
        out_shape=outs, compiler_params=_params(None), name="inproj_sample",
    )(x, g_mix, w_all, bf_row)


def _fox_prompt_kernel(q_ref, k_ref, v_ref, ct_ref, o_ref, *, tq):
    qi = pl.program_id(2)
    pair = pl.program_id(1)
    q = q_ref[0]
    lane = lax.broadcasted_iota(I32, (tq, LANES), 1)
    row = lax.broadcasted_iota(I32, (tq, tq), 0)
    colm = lax.broadcasted_iota(I32, (tq, tq), 1)
    outs = []
    for hh in range(HEADS_PER_LANE_TILE):
        in_head = (lane >= hh * HEAD_DIM) & (lane < (hh + 1) * HEAD_DIM)
        qh = jnp.where(in_head, q, jnp.zeros_like(q))
        head = pair * HEADS_PER_LANE_TILE + hh

        def tile(j, carry, masked):
            m, l, acc = carry
            start = pl.multiple_of(j * tq, tq)
            kj = k_ref[0, pl.ds(start, tq), :]
            vj = v_ref[0, pl.ds(start, tq), :]
            s = _dot_nt(qh, kj) - ct_ref[0, head, pl.ds(j, 1), :]
            if masked:
                s = jnp.where(colm <= row, s, -jnp.inf)
            m_new = jnp.maximum(m, jnp.max(s, axis=-1, keepdims=True))
            alpha = jnp.exp(m - m_new)
            p = jnp.exp(s - m_new)
            l = alpha * l + jnp.sum(p, axis=-1, keepdims=True)
            acc = alpha * acc + _dot(p.astype(BF16), vj)
            return m_new, l, acc

        init = (jnp.full((tq, 1), -jnp.inf, F32), jnp.zeros((tq, 1), F32), jnp.zeros((tq, LANES), F32))
        carry = lax.fori_loop(0, qi, lambda j, c: tile(j, c, False), init)
        _, l, acc = tile(qi, carry, True)
        outs.append(acc / l)
    o_ref[0] = jnp.where(lane < HEAD_DIM, outs[0], outs[1])


def _fox_prompt(q, kb, vb, ct, *, tq):
    b, s, aw = q.shape
    tq = min(tq, s)
    assert s % tq == 0
    n_pairs = aw // LANES
    ct = ct.reshape(b, ct.shape[1], s // tq, tq)
    return pl.pallas_call(
        functools.partial(_fox_prompt_kernel, tq=tq),
        grid=(b, n_pairs, s // tq),
        in_specs=[pl.BlockSpec((1, tq, LANES), lambda bi, p, qi: (bi, qi, p)),
                  pl.BlockSpec((1, s, LANES), lambda bi, p, qi: (bi, 0, p)),
                  pl.BlockSpec((1, s, LANES), lambda bi, p, qi: (bi, 0, p)),
                  pl.BlockSpec((1,) + ct.shape[1:], lambda bi, p, qi: (bi, 0, 0, 0))],
        out_specs=pl.BlockSpec((1, tq, LANES), lambda bi, p, qi: (bi, qi, p)),
        out_shape=jax.ShapeDtypeStruct((b, s, aw), F32),
        compiler_params=_params(("arbitrary", "arbitrary", "arbitrary")),
        name="fox_prompt",
    )(q, kb, vb, ct)


def _fox_decode_kernel(pt_ref, q_ref, kn_ref, vn_ref, lfn_ref, ustrict_ref, *rest, n_group, n_new):
    kt_refs = rest[:n_group]
    vt_refs = rest[n_group:2 * n_group]
    lf_refs = rest[2 * n_group:3 * n_group]
    o_ref, m_ref, l_ref, acc_ref, tail_ref = rest[3 * n_group:]
    j = pl.program_id(1)
    n_heads = lfn_ref.shape[1]
    width = q_ref.shape[2]
    rows = n_new * n_heads

    lane_head = lax.broadcasted_iota(I32, (n_heads, width), 1) // HEAD_DIM
    sub = lax.broadcasted_iota(I32, (n_heads, width), 0)
    hmask = lane_head == sub
    q = q_ref[0].astype(F32)
    qexp = jnp.concatenate([jnp.where(hmask, jnp.broadcast_to(q[t:t + 1], (n_heads, width)), 0.0)
                            for t in range(n_new)], axis=0)
    trow = lax.broadcasted_iota(I32, (rows, 1), 0) // n_heads

    @pl.when(j == 0)
    def _():
        lfn = lfn_ref[0]
        kn, vn = kn_ref[0], vn_ref[0]
        run = jnp.zeros((n_heads, 1), F32)
        logits = []
        for sp in range(n_new):
            run = run + lfn[:, sp:sp + 1]
            bias = jnp.concatenate([run] * n_new, axis=0)
            sc = jnp.sum(qexp * kn[sp:sp + 1], axis=-1, keepdims=True) - bias
            logits.append(jnp.where(trow >= sp, sc, -jnp.inf))
        m = logits[0]
        for sc in logits[1:]:
            m = jnp.maximum(m, sc)
        l = jnp.zeros((rows, 1), F32)
        acc = jnp.zeros((rows, width), F32)
        for sp in range(n_new):
            p = jnp.exp(logits[sp] - m)
            l = l + p
            acc = acc + p * vn[sp:sp + 1]
        m_ref[...] = m
        l_ref[...] = l
        acc_ref[...] = acc
        tail_ref[...] = jnp.zeros_like(tail_ref)

    qb = qexp.astype(BF16)
    tail = tail_ref[:, :1]
    scores = []
    for g in range(n_group):
        kt = kt_refs[g][0].reshape(width, -1).astype(BF16)
        lf = lf_refs[g][0]
        lf_pad = jnp.concatenate([lf, jnp.zeros_like(lf)], axis=0)
        suffix = _dot3(lf_pad, ustrict_ref[...])[:n_heads] + tail
        tail = tail + jnp.sum(lf, axis=-1, keepdims=True)
        scores.append(_dot(qb, kt) + jnp.concatenate([suffix] * n_new, axis=0))
    tail_ref[...] = jnp.broadcast_to(tail, tail_ref.shape)
    m_old = m_ref[...]
    m_new = m_old
    for s in scores:
        m_new = jnp.maximum(m_new, jnp.max(s, axis=-1, keepdims=True))
    alpha = jnp.exp(m_old - m_new)
    l = alpha * l_ref[...]
    acc = alpha * acc_ref[...]
    for g, s in enumerate(scores):
        p = jnp.exp(s - m_new)
        l = l + jnp.sum(p, axis=-1, keepdims=True)
        acc = acc + _dot_nt(p.astype(BF16), vt_refs[g][0].reshape(width, -1).astype(BF16))
    l_ref[...] = l
    acc_ref[...] = acc
    m_ref[...] = m_new

    @pl.when(j == pl.num_programs(1) - 1)
    def _():
        o = acc_ref[...] / l_ref[...]
        outs = [jnp.sum(jnp.where(hmask, o[t * n_heads:(t + 1) * n_heads], 0.0), axis=0, keepdims=True)
                for t in range(n_new)]
        o_ref[0] = jnp.concatenate(outs, axis=0)


def _fox_decode(page_table, q, k_new, v_new, lfn_t, kt_cache, vt_cache, lft_cache, *, n_group):
    bd, n_new, width = q.shape
    n_pages = page_table.shape[1]
    n_heads, page = lft_cache.shape[1:]
    n_group = min(n_group, n_pages)
    assert n_pages % n_group == 0
    n_steps = n_pages // n_group
    ii = lax.broadcasted_iota(I32, (page, page), 0)
    jj = lax.broadcasted_iota(I32, (page, page), 1)
    ustrict = (ii > jj).astype(BF16)

    def page_map(g, ndim):
        def index_map(b, j, pt):
            return (pt[b, n_pages - 1 - (j * n_group + g)],) + (0,) * ndim
        return index_map

    per_batch = lambda b, j, pt: (b, 0, 0)
    in_specs = [pl.BlockSpec((1, n_new, width), per_batch), pl.BlockSpec((1, n_new, width), per_batch),
                pl.BlockSpec((1, n_new, width), per_batch), pl.BlockSpec((1, n_heads, n_new), per_batch),
                pl.BlockSpec((page, page), lambda b, j, pt: (0, 0))]
    in_specs += [pl.BlockSpec((1,) + kt_cache.shape[1:], page_map(g, 3)) for g in range(n_group)]
    in_specs += [pl.BlockSpec((1,) + vt_cache.shape[1:], page_map(g, 3)) for g in range(n_group)]
    in_specs += [pl.BlockSpec((1, n_heads, page), page_map(g, 2)) for g in range(n_group)]
    rows = n_new * n_heads
    grid_spec = pltpu.PrefetchScalarGridSpec(
        num_scalar_prefetch=1, grid=(bd, n_steps), in_specs=in_specs,
        out_specs=pl.BlockSpec((1, n_new, width), per_batch),
        scratch_shapes=[pltpu.VMEM((rows, 1), F32), pltpu.VMEM((rows, 1), F32), pltpu.VMEM((rows, width), F32),
                        pltpu.VMEM((n_heads, LANES), F32)])
    return pl.pallas_call(
        functools.partial(_fox_decode_kernel, n_group=n_group, n_new=n_new),
        grid_spec=grid_spec, out_shape=jax.ShapeDtypeStruct((bd, n_new, width), F32),
        compiler_params=_params(("arbitrary", "arbitrary")), name="fox_decode",
    )(page_table, q, k_new, v_new, lfn_t, ustrict,
      *([kt_cache] * n_group), *([vt_cache] * n_group), *([lft_cache] * n_group))


def _ssm_kernel(u_ref, h0r_ref, h0i_ref, ar_ref, ai_ref, bd_ref, cdr_ref, cdi_ref, d_ref, wglu_ref, bglu_ref,
                gout_ref, o_ref, hr_ref, hi_ref, bu_ref, h_ref, *, n_batch, n_time):
    i = pl.program_id(0)
    half = ar_ref.shape[1]
    blk = half // SSM_LANE_BLOCKS
    rows = n_batch * n_time

    @pl.when(i == 0)
    def _():
        h_ref[:, :half] = h0r_ref[...]
        h_ref[:, half:] = h0i_ref[...]

    u = u_ref[...]
    ub = u.astype(BF16)
    for jb in range(SSM_LANE_BLOCKS):
        bu = _dot(ub[:, jb * LANES:(jb + 1) * LANES], bd_ref[jb])
        bu_ref[:, jb * blk:(jb + 1) * blk] = bu[:, :blk]
        bu_ref[:, half + jb * blk:half + (jb + 1) * blk] = bu[:, blk:]

    chunk = blk
    for bg in range(n_batch // SUBLANES):
        for c in range(half // chunk):
            lo_r, lo_i = c * chunk, half + c * chunk
            a_r = jnp.broadcast_to(ar_ref[:, lo_r:lo_r + chunk], (SUBLANES, chunk))
            a_i = jnp.broadcast_to(ai_ref[:, lo_r:lo_r + chunk], (SUBLANES, chunk))
            h_r0 = h_ref[bg * SUBLANES:(bg + 1) * SUBLANES, lo_r:lo_r + chunk]
            h_i0 = h_ref[bg * SUBLANES:(bg + 1) * SUBLANES, lo_i:lo_i + chunk]

            def step(t, carry, lo_r=lo_r, lo_i=lo_i, a_r=a_r, a_i=a_i, bg=bg):
                h_r, h_i = carry
                r0 = pl.multiple_of(t * n_batch + bg * SUBLANES, SUBLANES)
                n_r = a_r * h_r - a_i * h_i + bu_ref[pl.ds(r0, SUBLANES), lo_r:lo_r + chunk]
                n_i = a_r * h_i + a_i * h_r + bu_ref[pl.ds(r0, SUBLANES), lo_i:lo_i + chunk]
                bu_ref[pl.ds(r0, SUBLANES), lo_r:lo_r + chunk] = n_r
                bu_ref[pl.ds(r0, SUBLANES), lo_i:lo_i + chunk] = n_i
                return n_r, n_i

            h_r, h_i = lax.fori_loop(0, n_time, step, (h_r0, h_i0), unroll=min(4, n_time))
            h_ref[bg * SUBLANES:(bg + 1) * SUBLANES, lo_r:lo_r + chunk] = h_r
            h_ref[bg * SUBLANES:(bg + 1) * SUBLANES, lo_i:lo_i + chunk] = h_i

    ys = []
    for jb in range(SSM_LANE_BLOCKS):
        hre = bu_ref[:, jb * blk:(jb + 1) * blk].astype(BF16)
        him = bu_ref[:, half + jb * blk:half + (jb + 1) * blk].astype(BF16)
        ys.append(_dot(hre, cdr_ref[jb]) - _dot(him, cdi_ref[jb]))
    y = jnp.concatenate(ys, axis=-1) + d_ref[...] * u
    y = jax.nn.gelu(y, approximate=True)
    y = y * jax.nn.sigmoid(_dot(y.astype(BF16), wglu_ref[...]) + bglu_ref[...])
    o_ref[...] = _rms(y, gout_ref[...]).astype(BF16)

    @pl.when(i == pl.num_programs(0) - 1)
    def _():
        hr_ref[...] = h_ref[:, :half]
        hi_ref[...] = h_ref[:, half:]


def _ssm(u_tb, h0r, h0i, abar_re, abar_im, bd, cdr, cdi, d_row, w_glu, b_glu, g_out, *, n_batch, n_time):
    total, width = u_tb.shape
    half = abar_re.shape[1]
    rows = n_batch * n_time
    assert total % rows == 0 and n_batch % SUBLANES == 0
    const2 = lambda i: (0, 0)
    const3 = lambda i: (0, 0, 0)
    full = lambda a: pl.BlockSpec(a.shape, const2 if a.ndim == 2 else const3)
    return pl.pallas_call(
        functools.partial(_ssm_kernel, n_batch=n_batch, n_time=n_time),
        grid=(total // rows,),
        in_specs=[pl.BlockSpec((rows, width), lambda i: (i, 0))] + [full(a) for a in (
            h0r, h0i, abar_re, abar_im, bd, cdr, cdi, d_row, w_glu, b_glu, g_out)],
        out_specs=[pl.BlockSpec((rows, width), lambda i: (i, 0)), pl.BlockSpec((n_batch, half), const2),
                   pl.BlockSpec((n_batch, half), const2)],
        out_shape=[jax.ShapeDtypeStruct((total, width), BF16), jax.ShapeDtypeStruct((n_batch, half), F32),
                   jax.ShapeDtypeStruct((n_batch, half), F32)],
        scratch_shapes=[pltpu.VMEM((rows, 2 * half), F32), pltpu.VMEM((n_batch, 2 * half), F32)],
        compiler_params=_params(("arbitrary",)), name="ssm",
    )(u_tb, h0r, h0i, abar_re, abar_im, bd, cdr, cdi, d_row, w_glu, b_glu, g_out)


def _outproj_kernel(x_ref, a_ref, s_ref, ga_ref, woa_ref, wos_ref, gffn_ref, wr_ref, br_ref, lstrict_ref,
                    x1_ref, hm_ref, idx_ref, gate_ref, rank_ref, before_ref, cnt_ref, cnt_sc):
    i = pl.program_id(0)
    tm = x_ref.shape[0]

    @pl.when(i == 0)
    def _():
        cnt_sc[...] = jnp.zeros_like(cnt_sc)

    before_ref[0] = cnt_sc[...]

    an = _rms(a_ref[...], ga_ref[...]).astype(BF16)
    x1 = x_ref[...] + _dot(an, woa_ref[...]) + _dot(s_ref[...], wos_ref[...])
    x1_ref[...] = x1
    hm = _rms(x1, gffn_ref[...])
    hm_ref[...] = hm
    logits = _dot(hm.astype(BF16), wr_ref[...]) + br_ref[...]
    lane_i = lax.broadcasted_iota(I32, (tm, LANES), 1)
    lane = lane_i.astype(F32)
    work = jnp.where(lane_i < N_EXPERTS, logits, -jnp.inf)
    vals, idxs = [], []
    for _ in range(TOP_K):
        mx = jnp.max(work, axis=-1, keepdims=True)
        ix = jnp.min(jnp.where(work == mx, lane, float(LANES)), axis=-1, keepdims=True)
        vals.append(mx)
        idxs.append(ix)
        work = jnp.where(lane == ix, -jnp.inf, work)
    exps = [jnp.exp(v - vals[0]) for v in vals]
    denom = exps[0]
    for e in exps[1:]:
        denom = denom + e
    onehot = jnp.zeros((tm, LANES), F32)
    for ix in idxs:
        onehot = onehot + (lane == ix).astype(F32)
    before = _dot(lstrict_ref[...], onehot.astype(BF16)) + cnt_sc[:1, :]
    idx_out = jnp.zeros((tm, LANES), I32)
    gate_out = jnp.zeros((tm, LANES), F32)
    rank_out = jnp.zeros((tm, LANES), I32)
    for k in range(TOP_K):
        rk = jnp.sum(jnp.where(lane == idxs[k], before, 0.0), axis=-1, keepdims=True)
        idx_out = jnp.where(lane_i == k, idxs[k].astype(I32), idx_out)
        gate_out = jnp.where(lane_i == k, exps[k] / denom, gate_out)
        rank_out = jnp.where(lane_i == k, rk.astype(I32), rank_out)
    idx_ref[...] = idx_out
    gate_ref[...] = gate_out
    rank_ref[...] = rank_out
    cnt = cnt_sc[:1, :] + jnp.sum(onehot, axis=0, keepdims=True)
    cnt_sc[...] = jnp.broadcast_to(cnt, cnt_sc.shape)
    cnt_ref[...] = jnp.broadcast_to(cnt, cnt_ref.shape)


def _outproj(x, attn, ssm_n, g_attn, wo_a, wo_s, g_ffn, wr_pad, br_pad, *, tm):
    t, d = x.shape
    tm = min(tm, t)
    assert t % tm == 0
    ii = lax.broadcasted_iota(I32, (tm, tm), 0)
    jj = lax.broadcasted_iota(I32, (tm, tm), 1)
    lstrict = (jj < ii).astype(BF16)
    rowsp = lambda w: pl.BlockSpec((tm, w), lambda i: (i, 0))
    full = lambda a: pl.BlockSpec(a.shape, lambda i: (0, 0))
    aw = attn.shape[1]
    return pl.pallas_call(
        _outproj_kernel, grid=(t // tm,),
        in_specs=[rowsp(d), rowsp(aw), rowsp(ssm_n.shape[1])] + [full(a) for a in (
            g_attn, wo_a, wo_s, g_ffn, wr_pad, br_pad, lstrict)],
        out_specs=[rowsp(d), rowsp(d), rowsp(LANES), rowsp(LANES), rowsp(LANES),
                   pl.BlockSpec((1, SUBLANES, LANES), lambda i: (i, 0, 0)),
                   pl.BlockSpec((SUBLANES, LANES), lambda i: (0, 0))],
        out_shape=[jax.ShapeDtypeStruct((t, d), F32), jax.ShapeDtypeStruct((t, d), F32),
                   jax.ShapeDtypeStruct((t, LANES), I32), jax.ShapeDtypeStruct((t, LANES), F32),
                   jax.ShapeDtypeStruct((t, LANES), I32), jax.ShapeDtypeStruct((t // tm, SUBLANES, LANES), F32),
                   jax.ShapeDtypeStruct((SUBLANES, LANES), F32)],
        scratch_shapes=[pltpu.VMEM((SUBLANES, LANES), F32)],
        compiler_params=_params(("arbitrary",)), name="outproj_router",
    )(x, attn, ssm_n, g_attn, wo_a, wo_s, g_ffn, wr_pad, br_pad, lstrict)


def _dispatch_kernel(gseg_ref, lstart_ref, ngran_ref, total_ref, hm_ref, idx_ref, rank_ref, before_ref, lstartv_ref,
                     gsegv_ref, xs_ref, slot_ref, sorted_ref, zeros_ref, sem, *, chunk):
    i = pl.program_id(0)
    tm = hm_ref.shape[0]
    lane = lax.broadcasted_iota(I32, (tm, LANES), 1)
    idx = idx_ref[...]
    rank = rank_ref[...].astype(F32)
    local_off = lstartv_ref[0][:1] - before_ref[0][:1]
    global_off = gsegv_ref[0][:1] - before_ref[0][:1]
    lslots = jnp.full((tm, LANES), -1.0, F32)
    gslots = jnp.zeros((tm, LANES), I32)
    for k in range(TOP_K):
        sel = lane == idx[:, k:k + 1]
        r_k = rank[:, k:k + 1]
        ls = jnp.sum(jnp.where(sel, local_off, 0.0), axis=-1, keepdims=True) + r_k
        gs = jnp.sum(jnp.where(sel, global_off, 0.0), axis=-1, keepdims=True) + r_k
        lslots = jnp.where(lane == k, ls, lslots)
        gslots = jnp.where(lane == k, gs.astype(I32), gslots)
    slot_ref[...] = gslots
    lt = lslots.T
    hb = hm_ref[...].astype(BF16)
    for c in range(sorted_ref.shape[0] // chunk):
        pos = (c * chunk + lax.broadcasted_iota(I32, (chunk, tm), 0)).astype(F32)
        perm = jnp.zeros((chunk, tm), F32)
        for k in range(TOP_K):
            perm = perm + jnp.where(pos == lt[k:k + 1], 1.0, 0.0)
        sorted_ref[c * chunk:(c + 1) * chunk, :] = _dot(perm.astype(BF16), hb)

    def copy(src, dst):
        return pltpu.make_async_copy(sorted_ref.at[pl.ds(src, GRANULE)], xs_ref.at[pl.ds(dst, GRANULE)], sem.at[0])

    total = 0
    for e in range(N_EXPERTS):
        n = ngran_ref[i * N_EXPERTS + e]
        src0 = lstart_ref[i * N_EXPERTS + e]
        dst0 = gseg_ref[i * N_EXPERTS + e]

        def body(g, c, src0=src0, dst0=dst0):
            copy(pl.multiple_of(src0 + g * GRANULE, GRANULE), pl.multiple_of(dst0 + g * GRANULE, GRANULE)).start()
            return c
        lax.fori_loop(0, n, body, 0)
        total = total + n

    def wait_one(g, c):
        copy(0, 0).wait()
        return c
    lax.fori_loop(0, total, wait_one, 0)

    @pl.when(i == pl.num_programs(0) - 1)
    def _():
        zeros_ref[...] = jnp.zeros_like(zeros_ref)
        end = total_ref[0]

        def fill(g):
            dst = pl.multiple_of(end + g * GRANULE, GRANULE)
            return pltpu.make_async_copy(zeros_ref, xs_ref.at[pl.ds(dst, GRANULE)], sem.at[0])

        n_fill = (xs_ref.shape[0] - end) // GRANULE
        lax.fori_loop(0, n_fill, lambda g, c: (fill(g).start(), c)[1], 0)
        lax.fori_loop(0, n_fill, lambda g, c: (fill(g).wait(), c)[1], 0)


def _dispatch(route, hm, idx, rank, before, *, tm):
    t, d = hm.shape
    nt = t // tm
    n_local = TOP_K * tm + N_EXPERTS * GRANULE
    chunk = 256 if n_local % 256 == 0 else LANES
    assert n_local % chunk == 0
    rowsp = lambda w: pl.BlockSpec((tm, w), lambda i, *_: (i, 0))
    tile3 = pl.BlockSpec((1, SUBLANES, LANES), lambda i, *_: (i, 0, 0))
    grid_spec = pltpu.PrefetchScalarGridSpec(
        num_scalar_prefetch=4, grid=(nt,),
        in_specs=[rowsp(d), rowsp(LANES), rowsp(LANES), tile3, tile3, tile3],
        out_specs=[pl.BlockSpec(memory_space=pl.ANY), rowsp(LANES)],
        scratch_shapes=[pltpu.VMEM((n_local, d), F32), pltpu.VMEM((GRANULE, d), F32),
                        pltpu.SemaphoreType.DMA((1,))])
    return pl.pallas_call(
        functools.partial(_dispatch_kernel, chunk=chunk),
        grid_spec=grid_spec,
        out_shape=[jax.ShapeDtypeStruct((route["n_xs"], d), F32), jax.ShapeDtypeStruct((t, LANES), I32)],
        compiler_params=_params(("arbitrary",)), name="dispatch",
    )(route["gseg"], route["lstart"], route["ngran"], route["total"], hm, idx, rank, before,
      route["lstart_v"], route["gseg_v"])


def _experts_kernel(ie_ref, ib_ref, lo_ref, hi_ref, fl_ref, ni_ref, xs_ref, wup_ref, bup_ref, wdn_ref, bdn_ref,
                    ys_ref, wup_b, wdn_b, *, d_ff):
    w = pl.program_id(0)
    bm = xs_ref.shape[0]

    @pl.when(w >= ni_ref[0])
    def _():
        ys_ref[...] = jnp.zeros_like(ys_ref)

    @pl.when(w < ni_ref[0])
    def _():
        first = (fl_ref[w] & 1) == 1

        @pl.when((fl_ref[w] & 2) == 2)
        def _():
            step = 128
            for r in range(0, wup_b.shape[0], step):
                wup_b[r:r + step, :] = wup_ref[0, r:r + step, :].astype(BF16)
            for r in range(0, wdn_b.shape[0], step):
                wdn_b[r:r + step, :] = wdn_ref[0, r:r + step, :].astype(BF16)

        x = xs_ref[...].astype(BF16)
        up = _dot(x, wup_b[...]) + bup_ref[0]
        gate = jnp.minimum(up[:, :d_ff], SWIGLU_LIMIT)
        lin = jnp.clip(up[:, d_ff:], -SWIGLU_LIMIT, SWIGLU_LIMIT)
        act = gate * jax.nn.sigmoid(SWIGLU_ALPHA * gate) * (lin + 1.0)
        out = _dot(act.astype(BF16), wdn_b[...]) + bdn_ref[0]

        @pl.when(first)
        def _():
            ys_ref[...] = out

        @pl.when(jnp.logical_not(first))
        def _():
            rowi = lax.broadcasted_iota(I32, (bm, 1), 0)
            mine = (rowi >= lo_ref[w]) & (rowi < hi_ref[w])
            ys_ref[...] = jnp.where(mine, out, ys_ref[...])


def _experts(route, xs, w_up, b_up, w_down, b_down, *, bm):
    n_xs, d = xs.shape
    n_work = route["item_expert"].shape[0]
    d_ff = w_down.shape[1]
    by_block = lambda w, ie, ib, lo, hi, fl, ni: (ib[w], 0)
    by_expert = lambda w, ie, ib, lo, hi, fl, ni: (ie[w], 0, 0)
    grid_spec = pltpu.PrefetchScalarGridSpec(
        num_scalar_prefetch=6, grid=(n_work,),
        in_specs=[pl.BlockSpec((bm, d), by_block),
                  pl.BlockSpec((1,) + w_up.shape[1:], by_expert), pl.BlockSpec((1, 1, b_up.shape[2]), by_expert),
                  pl.BlockSpec((1,) + w_down.shape[1:], by_expert), pl.BlockSpec((1, 1, d), by_expert)],
        out_specs=pl.BlockSpec((bm, d), by_block),
        scratch_shapes=[pltpu.VMEM(w_up.shape[1:], BF16), pltpu.VMEM(w_down.shape[1:], BF16)])
    return pl.pallas_call(
        functools.partial(_experts_kernel, d_ff=d_ff),
        grid_spec=grid_spec, out_shape=jax.ShapeDtypeStruct((n_xs, d), F32),
        compiler_params=_params(("arbitrary",)), name="experts",
    )(route["item_expert"], route["item_block"], route["item_lo"], route["item_hi"], route["item_flags"],
      route["n_items"], xs, w_up, b_up, w_down, b_down)


def _route(counts, before, *, t, bm):
    experts = jnp.arange(N_EXPERTS, dtype=I32)
    cnt = counts.astype(I32)
    bef = before.astype(I32)
    nt = bef.shape[0]
    local = jnp.concatenate([bef[1:], cnt[None]], axis=0) - bef
    ngran = (local + GRANULE - 1) // GRANULE
    lpad = ngran * GRANULE
    lend = jnp.cumsum(lpad, axis=1)
    lstart = lend - lpad
    tot = jnp.sum(lpad, axis=0)
    end = jnp.cumsum(tot)
    start = end - tot
    gseg = start[None] + jnp.cumsum(lpad, axis=0) - lpad
    max_rows = t * TOP_K + nt * N_EXPERTS * (GRANULE - 1)
    n_blocks = -(-max_rows // bm) + 1
    n_xs = n_blocks * bm
    n_work = n_blocks + N_EXPERTS - 1
    first_blk = start // bm
    n_touch = jnp.where(tot > 0, (end - 1) // bm - first_blk + 1, 0)
    wend = jnp.cumsum(n_touch)
    wstart = wend - n_touch
    n_items = wend[-1]
    w_all = jnp.arange(n_work, dtype=I32)
    w = jnp.minimum(w_all, n_items - 1)
    e = jnp.minimum(jnp.sum((w[:, None] >= wend[None, :]).astype(I32), axis=1), N_EXPERTS - 1)
    onehot = (e[:, None] == experts[None, :]).astype(I32)
    pick = lambda v: jnp.sum(onehot * v[None, :], axis=1)
    blk = pick(first_blk) + (w - pick(wstart))
    blk = jnp.minimum(blk + (w_all - w), n_blocks - 1)
    lo = jnp.maximum(pick(start) - blk * bm, 0)
    hi = jnp.minimum(pick(end) - blk * bm, bm)
    prev = lambda v: jnp.concatenate([jnp.full((1,), -1, I32), v[:-1]])
    flags = (blk != prev(blk)).astype(I32) + 2 * (e != prev(e)).astype(I32)
    lanes = lambda v: jnp.pad(v.astype(F32), ((0, 0), (0, LANES - N_EXPERTS)))
    return dict(item_expert=e, item_block=blk, item_lo=lo, item_hi=hi, item_flags=flags,
                n_items=n_items.reshape(1), gseg=gseg.reshape(-1), lstart=lstart.reshape(-1),
                ngran=ngran.reshape(-1), total=end[-1:], n_xs=n_xs,
                lstart_v=jnp.broadcast_to(lanes(lstart)[:, None, :], (nt, SUBLANES, LANES)),
                gseg_v=jnp.broadcast_to(lanes(gseg)[:, None, :], (nt, SUBLANES, LANES)))


def _tail_kernel(slot_ref, slotn_ref, x1_ref, ys_ref, gate_ref, p_ref, gple_ref, wg_ref, wp_ref, gfin_ref, o_ref,
                 ybuf, sem):
    i = pl.program_id(0)
    tm = x1_ref.shape[0]
    buf = i % 2

    def gather(rows_ref, b):
        for k in range(TOP_K):
            def body(r, c, k=k):
                pltpu.make_async_copy(ys_ref.at[pl.ds(rows_ref[0, 0, k * tm + r], 1)], ybuf.at[b, k, pl.ds(r, 1)],
                                      sem.at[b]).start()
                return c
            lax.fori_loop(0, tm, body, 0, unroll=8)

    @pl.when(i == 0)
    def _():
        gather(slot_ref, 0)

    @pl.when(i + 1 < pl.num_programs(0))
    def _():
        gather(slotn_ref, 1 - buf)

    for k in range(TOP_K):
        pltpu.make_async_copy(ys_ref.at[pl.ds(0, tm)], ybuf.at[buf, k], sem.at[buf]).wait()
    gates = gate_ref[...]
    x2 = x1_ref[...]
    for k in range(TOP_K):
        x2 = x2 + gates[:, k:k + 1] * ybuf[buf, k]
    g = jax.nn.sigmoid(_dot(_rms(x2, gple_ref[...]).astype(BF16), wg_ref[...]))
    x3 = x2 + _dot(p_ref[...].astype(BF16), wp_ref[...]) * g
    o_ref[...] = _rms(x3, gfin_ref[...])


def _tail(slots, x1, ys, gates, p, g_ple, w_gate, w_proj, g_final, *, tm):
    t, d = x1.shape
    nt = t // tm
    rowsp = lambda w: pl.BlockSpec((tm, w), lambda i: (i, 0))
    full = lambda a: pl.BlockSpec(a.shape, lambda i: (0, 0))
    smem = lambda f: pl.BlockSpec((1, 1, TOP_K * tm), f, memory_space=pltpu.SMEM)
    return pl.pallas_call(
        _tail_kernel, grid=(nt,),
        in_specs=[smem(lambda i: (i, 0, 0)), smem(lambda i: (jnp.minimum(i + 1, nt - 1), 0, 0)), rowsp(d),
                  pl.BlockSpec(memory_space=pl.ANY), rowsp(LANES), rowsp(p.shape[1])]
        + [full(a) for a in (g_ple, w_gate, w_proj, g_final)],
        out_specs=rowsp(d), out_shape=jax.ShapeDtypeStruct((t, d), F32),
        scratch_shapes=[pltpu.VMEM((2, TOP_K, tm, d), F32), pltpu.SemaphoreType.DMA((2,))],
        compiler_params=_params(("arbitrary",)), name="tail",
    )(slots, slots, x1, ys, gates, p, g_ple, w_gate, w_proj, g_final)


def _block_diag(a, n):
    j, _, r, c = a.shape
    eye = jnp.eye(n, dtype=a.dtype)
    return jnp.einsum("ab,jarc->jarbc", eye, a).reshape(j, n * r, n * c)


def _mixer_tail(x_tok, attn, ssm_n, p_tok, lw, *, tm, bm):
    t, d = x_tok.shape
    tm = min(tm, t)
    assert t % tm == 0
    x1, hm, idx, gates, rank, before, counts = _outproj(
        x_tok, attn, ssm_n, lw["g_attn"], lw["wo_a"], lw["wo_s"], lw["g_ffn"], lw["wr_pad"], lw["br_pad"], tm=tm)
    route = _route(counts[0, :N_EXPERTS], before[:, 0, :N_EXPERTS], t=t, bm=bm)
    xs, slot = _dispatch(route, hm, idx, rank, before, tm=tm)
    ys = _experts(route, xs, lw["w_up"], lw["b_up"], lw["w_down"], lw["b_down"], bm=bm)
    nt = t // tm
    slots = jnp.transpose(slot[:, :TOP_K].reshape(nt, tm, TOP_K), (0, 2, 1)).reshape(nt, 1, TOP_K * tm)
    return _tail(slots, x1, ys, gates, p_tok, lw["g_ple"], lw["w_ple_gate"], lw["w_ple_proj"], lw["g_final"], tm=tm)


def kernel(x_prompt, x_sample, p_prompt, p_sample, cache_k, cache_v, cache_logf, state_ssm_re, state_ssm_im,
           page_table, g_mix, w_in, b_fgate, lam_re, lam_im, log_step, b_ssm_re, b_ssm_im, c_ssm_re, c_ssm_im,
           d_ssm, w_glu, b_glu, g_attn_out, g_ssm_out, w_out, g_ffn, w_router, b_router, w_up, b_up, w_down,
           b_down, g_ple, w_ple_gate, w_ple_proj, g_final):
    depth = w_in.shape[0]
    assert depth == 1
    bp, sp, d = x_prompt.shape
    bd, sd, _ = x_sample.shape
    n_heads = b_fgate.shape[1]
    aw = n_heads * HEAD_DIM
    n_groups, n_state = lam_re.shape[1:]
    sw = n_groups * SSM_GROUP
    half = n_groups * n_state
    assert sw == SSM_LANE_BLOCKS * LANES and w_in.shape[2] == 3 * aw + n_heads + sw

    wi = w_in[0]
    wq, wk, wv = wi[:, :aw], wi[:, aw:2 * aw], wi[:, 2 * aw:3 * aw]
    wf, wu = wi[:, 3 * aw:3 * aw + n_heads], wi[:, 3 * aw + n_heads:]
    w_all = jnp.concatenate([wq, wk, wv, wu, jnp.pad(wf, ((0, 0), (0, LANES - n_heads)))], axis=1).astype(BF16)
    wft = jnp.pad(wf.T, ((0, 2 * SUBLANES - n_heads), (0, 0))).astype(BF16)
    bf_col = jnp.pad(b_fgate[0][:, None], ((0, 2 * SUBLANES - n_heads), (0, 0)))
    bf_row = jnp.pad(b_fgate, ((0, 0), (0, LANES - n_heads)))

    abar_re, abar_im, bbr_t, bbi_t = _s5_params(
        lam_re[0], lam_im[0], log_step[0], jnp.transpose(b_ssm_re[0], (0, 2, 1)), jnp.transpose(b_ssm_im[0], (0, 2, 1)))
    gpb = n_groups // SSM_LANE_BLOCKS
    blk4 = lambda a: a.reshape((SSM_LANE_BLOCKS, gpb) + a.shape[1:])
    bd_mat = jnp.concatenate([_block_diag(blk4(bbr_t), gpb), _block_diag(blk4(bbi_t), gpb)], axis=-1).astype(BF16)
    cdr = _block_diag(blk4(jnp.transpose(c_ssm_re[0], (0, 2, 1))), gpb).astype(BF16)
    cdi = _block_diag(blk4(jnp.transpose(c_ssm_im[0], (0, 2, 1))), gpb).astype(BF16)
    ssm_w = (abar_re.reshape(1, half), abar_im.reshape(1, half), bd_mat, cdr, cdi, d_ssm[0].reshape(1, sw),
             w_glu[0].astype(BF16), b_glu, g_ssm_out)

    lw = dict(g_attn=g_attn_out, wo_a=w_out[0, :aw].astype(BF16), wo_s=w_out[0, aw:].astype(BF16), g_ffn=g_ffn,
              wr_pad=jnp.pad(w_router[0], ((0, 0), (0, LANES - N_EXPERTS))).astype(BF16),
              br_pad=jnp.pad(b_router, ((0, 0), (0, LANES - N_EXPERTS))),
              w_up=w_up[0], b_up=b_up[0][:, None, :], w_down=w_down[0],
              b_down=b_down[0][:, None, :], g_ple=g_ple, w_ple_gate=w_ple_gate[0].astype(BF16),
              w_ple_proj=w_ple_proj[0].astype(BF16), g_final=g_final.reshape(1, d))

    q, k, v, kb, vb, u, lft, ct = _inproj_prompt(x_prompt, g_mix, w_all, wft, bf_col, aw=aw, n_heads=n_heads, tm=512)
    attn = _fox_prompt(q, kb, vb, ct, tq=512)
    n_time = min(64, sp)
    zeros_h = jnp.zeros((bp, half), F32)
    ssm_tb, hr_p, hi_p = _ssm(jnp.transpose(u, (1, 0, 2)).reshape(sp * bp, sw), zeros_h, zeros_h, *ssm_w,
                              n_batch=bp, n_time=n_time)
    ssm_p = jnp.transpose(ssm_tb.reshape(sp, bp, sw), (1, 0, 2)).reshape(bp * sp, sw)
    y_p = _mixer_tail(x_prompt.reshape(bp * sp, d), attn.reshape(bp * sp, aw), ssm_p,
                      p_prompt[0].reshape(bp * sp, -1), lw, tm=512, bm=256)

    ts = bd * sd
    qs, ks, vs, us, lfs = _inproj_sample(x_sample.reshape(ts, d), g_mix, w_all, bf_row, aw=aw)
    lfs = lfs[:, :n_heads]
    kt_cache = jnp.transpose(cache_k[0], (0, 2, 3, 1))
    vt_cache = jnp.transpose(cache_v[0], (0, 2, 3, 1))
    lft_cache = jnp.transpose(cache_logf[0], (0, 2, 1))
    attn_s = _fox_decode(page_table, qs.reshape(bd, sd, aw), ks.reshape(bd, sd, aw), vs.reshape(bd, sd, aw),
                         jnp.transpose(lfs.reshape(bd, sd, n_heads), (0, 2, 1)), kt_cache, vt_cache, lft_cache,
                         n_group=8)
    ssm_s_tb, hr_s, hi_s = _ssm(jnp.transpose(us.reshape(bd, sd, sw), (1, 0, 2)).reshape(ts, sw),
                                state_ssm_re[0].reshape(bd, half), state_ssm_im[0].reshape(bd, half), *ssm_w,
                                n_batch=bd, n_time=sd)
    ssm_s = jnp.transpose(ssm_s_tb.reshape(sd, bd, sw), (1, 0, 2)).reshape(ts, sw)
    y_s = _mixer_tail(x_sample.reshape(ts, d), attn_s.reshape(ts, aw), ssm_s, p_sample[0].reshape(ts, -1), lw,
                      tm=128, bm=128)

    heads = lambda a, b_, s_: a.reshape(1, b_, s_, n_heads, HEAD_DIM)
    state = lambda a, b_: a.reshape(1, b_, n_groups, n_state)
    return (y_p.reshape(bp, sp, d), y_s.reshape(bd, sd, d),
            heads(k, bp, sp), heads(v, bp, sp), jnp.transpose(lft, (0, 2, 1))[None],
            state(hr_p, bp), state(hi_p, bp),
            heads(ks, bd, sd), heads(vs, bd, sd), lfs.reshape(1, bd, sd, n_heads),
            state(hr_s, bd), state(hi_s, bd))
```

```python
import functools

import jax
import jax.numpy as jnp
from jax import lax
from jax.experimental import pallas as pl
from jax.experimental.pallas import tpu as pltpu

F32 = jnp.float32
BF16 = jnp.bfloat16
I32 = jnp.int32

RMS_EPS = 1e-6
LOG2E = 1.4426950408889634
HEAD_DIM = 64
HEADS_PER_LANE_TILE = 2
LANES = 128
SUBLANES = 8
SSM_GROUP = 16
SSM_STATE = 64
SSM_LANE_BLOCKS = 4
N_EXPERTS = 32
TOP_K = 4
GRANULE = 8
SWIGLU_ALPHA = 1.702
SWIGLU_LIMIT = 7.0
VMEM_LIMIT = 56 * 1024 * 1024


def _params(sem, vmem=VMEM_LIMIT):
    return pltpu.CompilerParams(dimension_semantics=sem, vmem_limit_bytes=vmem)


def _rms(x, g):
    return x * lax.rsqrt(jnp.mean(x * x, axis=-1, keepdims=True) + RMS_EPS) * g


def _log_sigmoid(x):
    return jnp.minimum(x, 0.0) - jnp.log1p(jnp.exp(-jnp.abs(x)))


def _split3(x):
    hi = x.astype(BF16)
    r1 = x - hi.astype(F32)
    mid = r1.astype(BF16)
    lo = (r1 - mid.astype(F32)).astype(BF16)
    return hi, mid, lo


def _dot(a, b):
    return jnp.dot(a, b, preferred_element_type=F32)


def _dot_nt(a, b):
    return lax.dot_general(a, b, (((1,), (1,)), ((), ())), preferred_element_type=F32)


def _dot3(x, m):
    hi, mid, lo = _split3(x)
    return _dot(hi, m) + _dot(mid, m) + _dot(lo, m)


def _s5_params_kernel(lr_ref, li_ref, ls_ref, brt_ref, bit_ref, ar_ref, ai_ref, bbr_ref, bbi_ref):
    lr, li = lr_ref[...], li_ref[...]
    step = jnp.exp(ls_ref[...])
    mag = jnp.exp(lr * step)
    a_re, a_im = mag * jnp.cos(li * step), mag * jnp.sin(li * step)
    nr, ni = a_re - 1.0, a_im
    den = lr * lr + li * li
    coef_re = (nr * lr + ni * li) / den
    coef_im = (ni * lr - nr * li) / den
    ar_ref[...] = a_re
    ai_ref[...] = a_im
    br, bi = brt_ref[...], bit_ref[...]
    bbr_ref[...] = coef_re * br - coef_im * bi
    bbi_ref[...] = coef_re * bi + coef_im * br


def _s5_params(lam_re, lam_im, log_step, b_re_t, b_im_t):
    g, p = lam_re.shape
    c = b_re_t.shape[1]
    return pl.pallas_call(
        _s5_params_kernel,
        out_shape=(jax.ShapeDtypeStruct((g, 1, p), F32), jax.ShapeDtypeStruct((g, 1, p), F32),
                   jax.ShapeDtypeStruct((g, c, p), F32), jax.ShapeDtypeStruct((g, c, p), F32)),
        name="s5_params",
    )(lam_re.reshape(g, 1, p), lam_im.reshape(g, 1, p), log_step.reshape(g, 1, 1), b_re_t, b_im_t)


def _inproj_core(x, g_ref, w_ref, aw):
    h = _rms(x, g_ref[...]).astype(BF16)
    z = _dot(h, w_ref[...])
    return h, z


def _inproj_prompt_kernel(x_ref, g_ref, w_ref, wft_ref, bf_ref, utri_ref,
                          q_ref, k_ref, v_ref, kb_ref, vb_ref, u_ref, lft_ref, ct_ref, carry_ref, *, aw, scale):
    tm = x_ref.shape[1]
    h, z = _inproj_core(x_ref[0], g_ref, w_ref, aw)
    q_ref[0] = (z[:, :aw] * scale).astype(BF16)
    k = z[:, aw:2 * aw]
    v = z[:, 2 * aw:3 * aw]
    k_ref[0] = k
    v_ref[0] = v
    kb_ref[0] = k.astype(BF16)
    vb_ref[0] = v.astype(BF16)
    u_ref[0] = z[:, 3 * aw:4 * aw]
    ft = _dot_nt(wft_ref[...], h) + bf_ref[...]
    lft = _log_sigmoid(ft)
    n_heads = lft_ref.shape[1]
    lft_ref[0] = lft[:n_heads]

    @pl.when(pl.program_id(1) == 0)
    def _():
        carry_ref[...] = jnp.zeros_like(carry_ref)

    cs = _dot3(lft, utri_ref[...]) + carry_ref[:, :1]
    carry_ref[...] = jnp.broadcast_to(cs[:, tm - 1:tm], carry_ref.shape)
    ct_ref[0] = cs[:n_heads]


def _inproj_prompt(x, g_mix, w_all, wft, bf_col, *, aw, n_heads, tm):
    b, s, d = x.shape
    tm = min(tm, s)
    assert s % tm == 0
    ii = lax.broadcasted_iota(I32, (tm, tm), 0)
    jj = lax.broadcasted_iota(I32, (tm, tm), 1)
    utri = (ii <= jj).astype(BF16)
    row = lambda bi, si: (bi, si, 0)
    col = lambda bi, si: (bi, 0, si)
    const = lambda bi, si: (0, 0)
    wide = pl.BlockSpec((1, tm, aw), row)
    tall = pl.BlockSpec((1, n_heads, tm), col)
    return pl.pallas_call(
        functools.partial(_inproj_prompt_kernel, aw=aw, scale=HEAD_DIM ** -0.5 * LOG2E),
        grid=(b, s // tm),
        in_specs=[pl.BlockSpec((1, tm, d), row), pl.BlockSpec((1, d), const), pl.BlockSpec(w_all.shape, const),
                  pl.BlockSpec(wft.shape, const), pl.BlockSpec(bf_col.shape, const), pl.BlockSpec((tm, tm), const)],
        out_specs=[wide, wide, wide, wide, wide, wide, tall, tall],
        out_shape=[jax.ShapeDtypeStruct((b, s, aw), BF16), jax.ShapeDtypeStruct((b, s, aw), F32),
                   jax.ShapeDtypeStruct((b, s, aw), F32), jax.ShapeDtypeStruct((b, s, aw), BF16),
                   jax.ShapeDtypeStruct((b, s, aw), BF16), jax.ShapeDtypeStruct((b, s, aw), F32),
                   jax.ShapeDtypeStruct((b, n_heads, s), F32), jax.ShapeDtypeStruct((b, n_heads, s), F32)],
        scratch_shapes=[pltpu.VMEM((2 * SUBLANES, LANES), F32)],
        compiler_params=_params(("arbitrary", "arbitrary")),
        name="inproj_prompt",
    )(x, g_mix, w_all, wft, bf_col, utri)


def _inproj_sample_kernel(x_ref, g_ref, w_ref, bf_ref, q_ref, k_ref, v_ref, u_ref, lf_ref, *, aw, scale):
    _, z = _inproj_core(x_ref[...], g_ref, w_ref, aw)
    q_ref[...] = (z[:, :aw] * scale).astype(BF16)
    k_ref[...] = z[:, aw:2 * aw]
    v_ref[...] = z[:, 2 * aw:3 * aw]
    u_ref[...] = z[:, 3 * aw:4 * aw]
    lf_ref[...] = _log_sigmoid(z[:, 4 * aw:] + bf_ref[...])


def _inproj_sample(x, g_mix, w_all, bf_row, *, aw):
    t, d = x.shape
    outs = [jax.ShapeDtypeStruct((t, aw), BF16)] + [jax.ShapeDtypeStruct((t, aw), F32)] * 3
    outs.append(jax.ShapeDtypeStruct((t, LANES), F32))
    return pl.pallas_call(
        functools.partial(_inproj_sample_kernel, aw=aw, scale=HEAD_DIM ** -0.5),
        out_shape=outs, compiler_params=_params(None), name="inproj_sample",
    )(x, g_mix, w_all, bf_row)


def _fox_prompt_kernel(q_ref, k_ref, v_ref, ct_ref, o_ref, *, tq):
    qi = pl.program_id(2)
    pair = pl.program_id(1)
    q = q_ref[0]
    lane = lax.broadcasted_iota(I32, (tq, LANES), 1)
    row = lax.broadcasted_iota(I32, (tq, tq), 0)
    colm = lax.broadcasted_iota(I32, (tq, tq), 1)
    qhs = []
    for hh in range(HEADS_PER_LANE_TILE):
        in_head = (lane >= hh * HEAD_DIM) & (lane < (hh + 1) * HEAD_DIM)
        qhs.append(jnp.where(in_head, q, jnp.zeros_like(q)))

    def tile(j, carry, masked):
        start = pl.multiple_of(j * tq, tq)
        kj = k_ref[0, pl.ds(start, tq), :]
        vj = v_ref[0, pl.ds(start, tq), :]
        new = []
        for hh in range(HEADS_PER_LANE_TILE):
            m, l, acc = carry[hh]
            ck = ct_ref[0, pair * HEADS_PER_LANE_TILE + hh, pl.ds(j, 1), :] * LOG2E
            s = _dot_nt(qhs[hh], kj) - ck
            if masked:
                s = jnp.where(colm <= row, s, -jnp.inf)
            m_new = jnp.maximum(m, jnp.max(s, axis=-1, keepdims=True))
            alpha = jnp.exp2(m - m_new)
            p = jnp.exp2(s - m_new)
            l = alpha * l + jnp.sum(p, axis=-1, keepdims=True)
            acc = alpha * acc + _dot(p.astype(BF16), vj)
            new.append((m_new, l, acc))
        return tuple(new)

    init = tuple((jnp.full((tq, 1), -jnp.inf, F32), jnp.zeros((tq, 1), F32), jnp.zeros((tq, LANES), F32))
                 for _ in range(HEADS_PER_LANE_TILE))
    carry = lax.fori_loop(0, qi, lambda j, c: tile(j, c, False), init)
    (_, l0, acc0), (_, l1, acc1) = tile(qi, carry, True)
    o_ref[0] = jnp.where(lane < HEAD_DIM, acc0 / l0, acc1 / l1)


def _fox_prompt(q, kb, vb, ct, *, tq):
    b, s, aw = q.shape
    tq = min(tq, s)
    assert s % tq == 0
    n_pairs = aw // LANES
    ct = ct.reshape(b, ct.shape[1], s // tq, tq)
    return pl.pallas_call(
        functools.partial(_fox_prompt_kernel, tq=tq),
        grid=(b, n_pairs, s // tq),
        in_specs=[pl.BlockSpec((1, tq, LANES), lambda bi, p, qi: (bi, qi, p)),
                  pl.BlockSpec((1, s, LANES), lambda bi, p, qi: (bi, 0, p)),
                  pl.BlockSpec((1, s, LANES), lambda bi, p, qi: (bi, 0, p)),
                  pl.BlockSpec((1,) + ct.shape[1:], lambda bi, p, qi: (bi, 0, 0, 0))],
        out_specs=pl.BlockSpec((1, tq, LANES), lambda bi, p, qi: (bi, qi, p)),
        out_shape=jax.ShapeDtypeStruct((b, s, aw), F32),
        compiler_params=_params(("arbitrary", "arbitrary", "arbitrary")),
        name="fox_prompt",
    )(q, kb, vb, ct)


def _fox_decode_kernel(pt_ref, q_ref, kn_ref, vn_ref, lfn_ref, ustrict_ref, *rest, n_group, n_new):
    kt_refs = rest[:n_group]
    vt_refs = rest[n_group:2 * n_group]
    lf_refs = rest[2 * n_group:3 * n_group]
    o_ref, m_ref, l_ref, acc_ref, tail_ref = rest[3 * n_group:]
    j = pl.program_id(1)
    n_heads = lfn_ref.shape[1]
    width = q_ref.shape[2]
    rows = n_new * n_heads

    lane_head = lax.broadcasted_iota(I32, (n_heads, width), 1) // HEAD_DIM
    sub = lax.broadcasted_iota(I32, (n_heads, width), 0)
    hmask = lane_head == sub
    q = q_ref[0].astype(F32)
    qexp = jnp.concatenate([jnp.where(hmask, jnp.broadcast_to(q[t:t + 1], (n_heads, width)), 0.0)
                            for t in range(n_new)], axis=0)
    trow = lax.broadcasted_iota(I32, (rows, 1), 0) // n_heads

    @pl.when(j == 0)
    def _():
        lfn = lfn_ref[0]
        kn, vn = kn_ref[0], vn_ref[0]
        run = jnp.zeros((n_heads, 1), F32)
        logits = []
        for sp in range(n_new):
            run = run + lfn[:, sp:sp + 1]
            bias = jnp.concatenate([run] * n_new, axis=0)
            sc = jnp.sum(qexp * kn[sp:sp + 1], axis=-1, keepdims=True) - bias
            logits.append(jnp.where(trow >= sp, sc, -jnp.inf))
        m = logits[0]
        for sc in logits[1:]:
            m = jnp.maximum(m, sc)
        l = jnp.zeros((rows, 1), F32)
        acc = jnp.zeros((rows, width), F32)
        for sp in range(n_new):
            p = jnp.exp(logits[sp] - m)
            l = l + p
            acc = acc + p * vn[sp:sp + 1]
        m_ref[...] = m
        l_ref[...] = l
        acc_ref[...] = acc
        tail_ref[...] = jnp.zeros_like(tail_ref)

    qb = qexp.astype(BF16)
    tail = tail_ref[:, :1]
    scores = []
    for g in range(n_group):
        kt = kt_refs[g][0].reshape(width, -1).astype(BF16)
        lf = lf_refs[g][0]
        lf_pad = jnp.concatenate([lf, jnp.zeros_like(lf)], axis=0)
        suffix = _dot3(lf_pad, ustrict_ref[...])[:n_heads] + tail
        tail = tail + jnp.sum(lf, axis=-1, keepdims=True)
        scores.append(_dot(qb, kt) + jnp.concatenate([suffix] * n_new, axis=0))
    tail_ref[...] = jnp.broadcast_to(tail, tail_ref.shape)
    m_old = m_ref[...]
    m_new = m_old
    for s in scores:
        m_new = jnp.maximum(m_new, jnp.max(s, axis=-1, keepdims=True))
    alpha = jnp.exp(m_old - m_new)
    l = alpha * l_ref[...]
    acc = alpha * acc_ref[...]
    for g, s in enumerate(scores):
        p = jnp.exp(s - m_new)
        l = l + jnp.sum(p, axis=-1, keepdims=True)
        acc = acc + _dot_nt(p.astype(BF16), vt_refs[g][0].reshape(width, -1).astype(BF16))
    l_ref[...] = l
    acc_ref[...] = acc
    m_ref[...] = m_new

    @pl.when(j == pl.num_programs(1) - 1)
    def _():
        o = acc_ref[...] / l_ref[...]
        outs = [jnp.sum(jnp.where(hmask, o[t * n_heads:(t + 1) * n_heads], 0.0), axis=0, keepdims=True)
                for t in range(n_new)]
        o_ref[0] = jnp.concatenate(outs, axis=0)


def _fox_decode(page_table, q, k_new, v_new, lfn_t, kt_cache, vt_cache, lft_cache, *, n_group):
    bd, n_new, width = q.shape
    n_pages = page_table.shape[1]
    n_heads, page = lft_cache.shape[1:]
    n_group = min(n_group, n_pages)
    assert n_pages % n_group == 0
    n_steps = n_pages // n_group
    ii = lax.broadcasted_iota(I32, (page, page), 0)
    jj = lax.broadcasted_iota(I32, (page, page), 1)
    ustrict = (ii > jj).astype(BF16)

    def page_map(g, ndim):
        def index_map(b, j, pt):
            return (pt[b, n_pages - 1 - (j * n_group + g)],) + (0,) * ndim
        return index_map

    per_batch = lambda b, j, pt: (b, 0, 0)
    in_specs = [pl.BlockSpec((1, n_new, width), per_batch), pl.BlockSpec((1, n_new, width), per_batch),
                pl.BlockSpec((1, n_new, width), per_batch), pl.BlockSpec((1, n_heads, n_new), per_batch),
                pl.BlockSpec((page, page), lambda b, j, pt: (0, 0))]
    in_specs += [pl.BlockSpec((1,) + kt_cache.shape[1:], page_map(g, 3)) for g in range(n_group)]
    in_specs += [pl.BlockSpec((1,) + vt_cache.shape[1:], page_map(g, 3)) for g in range(n_group)]
    in_specs += [pl.BlockSpec((1, n_heads, page), page_map(g, 2)) for g in range(n_group)]
    rows = n_new * n_heads
    grid_spec = pltpu.PrefetchScalarGridSpec(
        num_scalar_prefetch=1, grid=(bd, n_steps), in_specs=in_specs,
        out_specs=pl.BlockSpec((1, n_new, width), per_batch),
        scratch_shapes=[pltpu.VMEM((rows, 1), F32), pltpu.VMEM((rows, 1), F32), pltpu.VMEM((rows, width), F32),
                        pltpu.VMEM((n_heads, LANES), F32)])
    return pl.pallas_call(
        functools.partial(_fox_decode_kernel, n_group=n_group, n_new=n_new),
        grid_spec=grid_spec, out_shape=jax.ShapeDtypeStruct((bd, n_new, width), F32),
        compiler_params=_params(("arbitrary", "arbitrary")), name="fox_decode",
    )(page_table, q, k_new, v_new, lfn_t, ustrict,
      *([kt_cache] * n_group), *([vt_cache] * n_group), *([lft_cache] * n_group))


def _ssm_kernel(u_ref, h0r_ref, h0i_ref, ar_ref, ai_ref, bd_ref, cdr_ref, cdi_ref, d_ref, wglu_ref, bglu_ref,
                gout_ref, o_ref, hr_ref, hi_ref, bu_ref, h_ref, *, n_batch, n_time):
    i = pl.program_id(0)
    half = ar_ref.shape[1]
    blk = half // SSM_LANE_BLOCKS
    rows = n_batch * n_time

    @pl.when(i == 0)
    def _():
        h_ref[:, :half] = h0r_ref[...]
        h_ref[:, half:] = h0i_ref[...]

    u = u_ref[...]
    ub = u.astype(BF16)
    for jb in range(SSM_LANE_BLOCKS):
        bu = _dot(ub[:, jb * LANES:(jb + 1) * LANES], bd_ref[jb])
        bu_ref[:, jb * blk:(jb + 1) * blk] = bu[:, :blk]
        bu_ref[:, half + jb * blk:half + (jb + 1) * blk] = bu[:, blk:]

    chunk = blk
    for bg in range(n_batch // SUBLANES):
        for c in range(half // chunk):
            lo_r, lo_i = c * chunk, half + c * chunk
            a_r = jnp.broadcast_to(ar_ref[:, lo_r:lo_r + chunk], (SUBLANES, chunk))
            a_i = jnp.broadcast_to(ai_ref[:, lo_r:lo_r + chunk], (SUBLANES, chunk))
            h_r0 = h_ref[bg * SUBLANES:(bg + 1) * SUBLANES, lo_r:lo_r + chunk]
            h_i0 = h_ref[bg * SUBLANES:(bg + 1) * SUBLANES, lo_i:lo_i + chunk]

            def step(t, carry, lo_r=lo_r, lo_i=lo_i, a_r=a_r, a_i=a_i, bg=bg):
                h_r, h_i = carry
                r0 = pl.multiple_of(t * n_batch + bg * SUBLANES, SUBLANES)
                n_r = a_r * h_r - a_i * h_i + bu_ref[pl.ds(r0, SUBLANES), lo_r:lo_r + chunk]
                n_i = a_r * h_i + a_i * h_r + bu_ref[pl.ds(r0, SUBLANES), lo_i:lo_i + chunk]
                bu_ref[pl.ds(r0, SUBLANES), lo_r:lo_r + chunk] = n_r
                bu_ref[pl.ds(r0, SUBLANES), lo_i:lo_i + chunk] = n_i
                return n_r, n_i

            h_r, h_i = lax.fori_loop(0, n_time, step, (h_r0, h_i0), unroll=min(4, n_time))
            h_ref[bg * SUBLANES:(bg + 1) * SUBLANES, lo_r:lo_r + chunk] = h_r
            h_ref[bg * SUBLANES:(bg + 1) * SUBLANES, lo_i:lo_i + chunk] = h_i

    ys = []
    for jb in range(SSM_LANE_BLOCKS):
        hre = bu_ref[:, jb * blk:(jb + 1) * blk].astype(BF16)
        him = bu_ref[:, half + jb * blk:half + (jb + 1) * blk].astype(BF16)
        ys.append(_dot(hre, cdr_ref[jb]) - _dot(him, cdi_ref[jb]))
    y = jnp.concatenate(ys, axis=-1) + d_ref[...] * u
    y = jax.nn.gelu(y, approximate=True)
    y = y * jax.nn.sigmoid(_dot(y.astype(BF16), wglu_ref[...]) + bglu_ref[...])
    o_ref[...] = _rms(y, gout_ref[...]).astype(BF16)

    @pl.when(i == pl.num_programs(0) - 1)
    def _():
        hr_ref[...] = h_ref[:, :half]
        hi_ref[...] = h_ref[:, half:]


def _ssm(u_tb, h0r, h0i, abar_re, abar_im, bd, cdr, cdi, d_row, w_glu, b_glu, g_out, *, n_batch, n_time):
    total, width = u_tb.shape
    half = abar_re.shape[1]
    rows = n_batch * n_time
    assert total % rows == 0 and n_batch % SUBLANES == 0
    const2 = lambda i: (0, 0)
    const3 = lambda i: (0, 0, 0)
    full = lambda a: pl.BlockSpec(a.shape, const2 if a.ndim == 2 else const3)
    return pl.pallas_call(
        functools.partial(_ssm_kernel, n_batch=n_batch, n_time=n_time),
        grid=(total // rows,),
        in_specs=[pl.BlockSpec((rows, width), lambda i: (i, 0))] + [full(a) for a in (
            h0r, h0i, abar_re, abar_im, bd, cdr, cdi, d_row, w_glu, b_glu, g_out)],
        out_specs=[pl.BlockSpec((rows, width), lambda i: (i, 0)), pl.BlockSpec((n_batch, half), const2),
                   pl.BlockSpec((n_batch, half), const2)],
        out_shape=[jax.ShapeDtypeStruct((total, width), BF16), jax.ShapeDtypeStruct((n_batch, half), F32),
                   jax.ShapeDtypeStruct((n_batch, half), F32)],
        scratch_shapes=[pltpu.VMEM((rows, 2 * half), F32), pltpu.VMEM((n_batch, 2 * half), F32)],
        compiler_params=_params(("arbitrary",)), name="ssm",
    )(u_tb, h0r, h0i, abar_re, abar_im, bd, cdr, cdi, d_row, w_glu, b_glu, g_out)


def _outproj_kernel(x_ref, a_ref, s_ref, ga_ref, woa_ref, wos_ref, gffn_ref, wr_ref, br_ref, lstrict_ref,
                    x1_ref, hm_ref, idx_ref, gate_ref, rank_ref, before_ref, cnt_ref, cnt_sc):
    i = pl.program_id(0)
    tm = x_ref.shape[0]

    @pl.when(i == 0)
    def _():
        cnt_sc[...] = jnp.zeros_like(cnt_sc)

    before_ref[0] = cnt_sc[...]

    an = _rms(a_ref[...], ga_ref[...]).astype(BF16)
    x1 = x_ref[...] + _dot(an, woa_ref[...]) + _dot(s_ref[...], wos_ref[...])
    x1_ref[...] = x1
    hm = _rms(x1, gffn_ref[...])
    hm_ref[...] = hm
    logits = _dot(hm.astype(BF16), wr_ref[...]) + br_ref[...]
    lane_i = lax.broadcasted_iota(I32, (tm, LANES), 1)
    lane = lane_i.astype(F32)
    work = jnp.where(lane_i < N_EXPERTS, logits, -jnp.inf)
    vals, idxs = [], []
    for _ in range(TOP_K):
        mx = jnp.max(work, axis=-1, keepdims=True)
        ix = jnp.min(jnp.where(work == mx, lane, float(LANES)), axis=-1, keepdims=True)
        vals.append(mx)
        idxs.append(ix)
        work = jnp.where(lane == ix, -jnp.inf, work)
    exps = [jnp.exp(v - vals[0]) for v in vals]
    denom = exps[0]
    for e in exps[1:]:
        denom = denom + e
    onehot = jnp.zeros((tm, LANES), F32)
    for ix in idxs:
        onehot = onehot + (lane == ix).astype(F32)
    before = _dot(lstrict_ref[...], onehot.astype(BF16)) + cnt_sc[:1, :]
    idx_out = jnp.zeros((tm, LANES), I32)
    gate_out = jnp.zeros((tm, LANES), F32)
    rank_out = jnp.zeros((tm, LANES), I32)
    for k in range(TOP_K):
        rk = jnp.sum(jnp.where(lane == idxs[k], before, 0.0), axis=-1, keepdims=True)
        idx_out = jnp.where(lane_i == k, idxs[k].astype(I32), idx_out)
        gate_out = jnp.where(lane_i == k, exps[k] / denom, gate_out)
        rank_out = jnp.where(lane_i == k, rk.astype(I32), rank_out)
    idx_ref[...] = idx_out
    gate_ref[...] = gate_out
    rank_ref[...] = rank_out
    cnt = cnt_sc[:1, :] + jnp.sum(onehot, axis=0, keepdims=True)
    cnt_sc[...] = jnp.broadcast_to(cnt, cnt_sc.shape)
    cnt_ref[...] = jnp.broadcast_to(cnt, cnt_ref.shape)


def _outproj(x, attn, ssm_n, g_attn, wo_a, wo_s, g_ffn, wr_pad, br_pad, *, tm):
    t, d = x.shape
    tm = min(tm, t)
    assert t % tm == 0
    ii = lax.broadcasted_iota(I32, (tm, tm), 0)
    jj = lax.broadcasted_iota(I32, (tm, tm), 1)
    lstrict = (jj < ii).astype(BF16)
    rowsp = lambda w: pl.BlockSpec((tm, w), lambda i: (i, 0))
    full = lambda a: pl.BlockSpec(a.shape, lambda i: (0, 0))
    aw = attn.shape[1]
    return pl.pallas_call(
        _outproj_kernel, grid=(t // tm,),
        in_specs=[rowsp(d), rowsp(aw), rowsp(ssm_n.shape[1])] + [full(a) for a in (
            g_attn, wo_a, wo_s, g_ffn, wr_pad, br_pad, lstrict)],
        out_specs=[rowsp(d), rowsp(d), rowsp(LANES), rowsp(LANES), rowsp(LANES),
                   pl.BlockSpec((1, SUBLANES, LANES), lambda i: (i, 0, 0)),
                   pl.BlockSpec((SUBLANES, LANES), lambda i: (0, 0))],
        out_shape=[jax.ShapeDtypeStruct((t, d), F32), jax.ShapeDtypeStruct((t, d), F32),
                   jax.ShapeDtypeStruct((t, LANES), I32), jax.ShapeDtypeStruct((t, LANES), F32),
                   jax.ShapeDtypeStruct((t, LANES), I32), jax.ShapeDtypeStruct((t // tm, SUBLANES, LANES), F32),
                   jax.ShapeDtypeStruct((SUBLANES, LANES), F32)],
        scratch_shapes=[pltpu.VMEM((SUBLANES, LANES), F32)],
        compiler_params=_params(("arbitrary",)), name="outproj_router",
    )(x, attn, ssm_n, g_attn, wo_a, wo_s, g_ffn, wr_pad, br_pad, lstrict)


def _dispatch_kernel(gseg_ref, lstart_ref, ngran_ref, total_ref, hm_ref, idx_ref, rank_ref, before_ref, lstartv_ref,
                     gsegv_ref, xs_ref, slot_ref, sorted_ref, zeros_ref, sem, *, chunk):
    i = pl.program_id(0)
    tm = hm_ref.shape[0]
    lane = lax.broadcasted_iota(I32, (tm, LANES), 1)
    idx = idx_ref[...]
    rank = rank_ref[...].astype(F32)
    local_off = lstartv_ref[0][:1] - before_ref[0][:1]
    global_off = gsegv_ref[0][:1] - before_ref[0][:1]
    lslots = jnp.full((tm, LANES), -1.0, F32)
    gslots = jnp.zeros((tm, LANES), I32)
    for k in range(TOP_K):
        sel = lane == idx[:, k:k + 1]
        r_k = rank[:, k:k + 1]
        ls = jnp.sum(jnp.where(sel, local_off, 0.0), axis=-1, keepdims=True) + r_k
        gs = jnp.sum(jnp.where(sel, global_off, 0.0), axis=-1, keepdims=True) + r_k
        lslots = jnp.where(lane == k, ls, lslots)
        gslots = jnp.where(lane == k, gs.astype(I32), gslots)
    slot_ref[...] = gslots
    lt = lslots.T
    hb = hm_ref[...].astype(BF16)
    for c in range(sorted_ref.shape[0] // chunk):
        pos = (c * chunk + lax.broadcasted_iota(I32, (chunk, tm), 0)).astype(F32)
        perm = jnp.zeros((chunk, tm), F32)
        for k in range(TOP_K):
            perm = perm + jnp.where(pos == lt[k:k + 1], 1.0, 0.0)
        sorted_ref[c * chunk:(c + 1) * chunk, :] = _dot(perm.astype(BF16), hb)

    def copy(src, dst):
        return pltpu.make_async_copy(sorted_ref.at[pl.ds(src, GRANULE)], xs_ref.at[pl.ds(dst, GRANULE)], sem.at[0])

    total = 0
    for e in range(N_EXPERTS):
        n = ngran_ref[i * N_EXPERTS + e]
        src0 = lstart_ref[i * N_EXPERTS + e]
        dst0 = gseg_ref[i * N_EXPERTS + e]

        def body(g, c, src0=src0, dst0=dst0):
            copy(pl.multiple_of(src0 + g * GRANULE, GRANULE), pl.multiple_of(dst0 + g * GRANULE, GRANULE)).start()
            return c
        lax.fori_loop(0, n, body, 0)
        total = total + n

    def wait_one(g, c):
        copy(0, 0).wait()
        return c
    lax.fori_loop(0, total, wait_one, 0)

    @pl.when(i == pl.num_programs(0) - 1)
    def _():
        zeros_ref[...] = jnp.zeros_like(zeros_ref)
        end = total_ref[0]

        def fill(g):
            dst = pl.multiple_of(end + g * GRANULE, GRANULE)
            return pltpu.make_async_copy(zeros_ref, xs_ref.at[pl.ds(dst, GRANULE)], sem.at[0])

        n_fill = (xs_ref.shape[0] - end) // GRANULE
        lax.fori_loop(0, n_fill, lambda g, c: (fill(g).start(), c)[1], 0)
        lax.fori_loop(0, n_fill, lambda g, c: (fill(g).wait(), c)[1], 0)


def _dispatch(route, hm, idx, rank, before, *, tm):
    t, d = hm.shape
    nt = t // tm
    n_local = TOP_K * tm + N_EXPERTS * GRANULE
    chunk = 256 if n_local % 256 == 0 else LANES
    assert n_local % chunk == 0
    rowsp = lambda w: pl.BlockSpec((tm, w), lambda i, *_: (i, 0))
    tile3 = pl.BlockSpec((1, SUBLANES, LANES), lambda i, *_: (i, 0, 0))
    grid_spec = pltpu.PrefetchScalarGridSpec(
        num_scalar_prefetch=4, grid=(nt,),
        in_specs=[rowsp(d), rowsp(LANES), rowsp(LANES), tile3, tile3, tile3],
        out_specs=[pl.BlockSpec(memory_space=pl.ANY), rowsp(LANES)],
        scratch_shapes=[pltpu.VMEM((n_local, d), F32), pltpu.VMEM((GRANULE, d), F32),
                        pltpu.SemaphoreType.DMA((1,))])
    return pl.pallas_call(
        functools.partial(_dispatch_kernel, chunk=chunk),
        grid_spec=grid_spec,
        out_shape=[jax.ShapeDtypeStruct((route["n_xs"], d), F32), jax.ShapeDtypeStruct((t, LANES), I32)],
        compiler_params=_params(("arbitrary",)), name="dispatch",
    )(route["gseg"], route["lstart"], route["ngran"], route["total"], hm, idx, rank, before,
      route["lstart_v"], route["gseg_v"])


def _experts_kernel(ie_ref, ib_ref, lo_ref, hi_ref, fl_ref, ni_ref, xs_ref, wup_ref, bup_ref, wdn_ref, bdn_ref,
                    ys_ref, wup_b, wdn_b, *, d_ff):
    w = pl.program_id(0)
    bm = xs_ref.shape[0]

    @pl.when(w >= ni_ref[0])
    def _():
        ys_ref[...] = jnp.zeros_like(ys_ref)

    @pl.when(w < ni_ref[0])
    def _():
        first = (fl_ref[w] & 1) == 1

        @pl.when((fl_ref[w] & 2) == 2)
        def _():
            step = 128
            for r in range(0, wup_b.shape[0], step):
                wup_b[r:r + step, :] = wup_ref[0, r:r + step, :].astype(BF16)
            for r in range(0, wdn_b.shape[0], step):
                wdn_b[r:r + step, :] = wdn_ref[0, r:r + step, :].astype(BF16)

        x = xs_ref[...].astype(BF16)
        up = _dot(x, wup_b[...]) + bup_ref[0]
        gate = jnp.minimum(up[:, :d_ff], SWIGLU_LIMIT)
        lin = jnp.clip(up[:, d_ff:], -SWIGLU_LIMIT, SWIGLU_LIMIT)
        act = gate * jax.nn.sigmoid(SWIGLU_ALPHA * gate) * (lin + 1.0)
        out = _dot(act.astype(BF16), wdn_b[...]) + bdn_ref[0]

        @pl.when(first)
        def _():
            ys_ref[...] = out

        @pl.when(jnp.logical_not(first))
        def _():
            rowi = lax.broadcasted_iota(I32, (bm, 1), 0)
            mine = (rowi >= lo_ref[w]) & (rowi < hi_ref[w])
            ys_ref[...] = jnp.where(mine, out, ys_ref[...])


def _experts(route, xs, w_up, b_up, w_down, b_down, *, bm):
    n_xs, d = xs.shape
    n_work = route["item_expert"].shape[0]
    d_ff = w_down.shape[1]
    by_block = lambda w, ie, ib, lo, hi, fl, ni: (ib[w], 0)
    by_expert = lambda w, ie, ib, lo, hi, fl, ni: (ie[w], 0, 0)
    grid_spec = pltpu.PrefetchScalarGridSpec(
        num_scalar_prefetch=6, grid=(n_work,),
        in_specs=[pl.BlockSpec((bm, d), by_block),
                  pl.BlockSpec((1,) + w_up.shape[1:], by_expert), pl.BlockSpec((1, 1, b_up.shape[2]), by_expert),
                  pl.BlockSpec((1,) + w_down.shape[1:], by_expert), pl.BlockSpec((1, 1, d), by_expert)],
        out_specs=pl.BlockSpec((bm, d), by_block),
        scratch_shapes=[pltpu.VMEM(w_up.shape[1:], BF16), pltpu.VMEM(w_down.shape[1:], BF16)])
    return pl.pallas_call(
        functools.partial(_experts_kernel, d_ff=d_ff),
        grid_spec=grid_spec, out_shape=jax.ShapeDtypeStruct((n_xs, d), F32),
        compiler_params=_params(("arbitrary",)), name="experts",
    )(route["item_expert"], route["item_block"], route["item_lo"], route["item_hi"], route["item_flags"],
      route["n_items"], xs, w_up, b_up, w_down, b_down)


def _route(counts, before, *, t, bm):
    experts = jnp.arange(N_EXPERTS, dtype=I32)
    cnt = counts.astype(I32)
    bef = before.astype(I32)
    nt = bef.shape[0]
    local = jnp.concatenate([bef[1:], cnt[None]], axis=0) - bef
    ngran = (local + GRANULE - 1) // GRANULE
    lpad = ngran * GRANULE
    lend = jnp.cumsum(lpad, axis=1)
    lstart = lend - lpad
    tot = jnp.sum(lpad, axis=0)
    end = jnp.cumsum(tot)
    start = end - tot
    gseg = start[None] + jnp.cumsum(lpad, axis=0) - lpad
    max_rows = t * TOP_K + nt * N_EXPERTS * (GRANULE - 1)
    n_blocks = -(-max_rows // bm) + 1
    n_xs = n_blocks * bm
    n_work = n_blocks + N_EXPERTS - 1
    first_blk = start // bm
    n_touch = jnp.where(tot > 0, (end - 1) // bm - first_blk + 1, 0)
    wend = jnp.cumsum(n_touch)
    wstart = wend - n_touch
    n_items = wend[-1]
    w_all = jnp.arange(n_work, dtype=I32)
    w = jnp.minimum(w_all, n_items - 1)
    e = jnp.minimum(jnp.sum((w[:, None] >= wend[None, :]).astype(I32), axis=1), N_EXPERTS - 1)
    onehot = (e[:, None] == experts[None, :]).astype(I32)
    pick = lambda v: jnp.sum(onehot * v[None, :], axis=1)
    blk = pick(first_blk) + (w - pick(wstart))
    blk = jnp.minimum(blk + (w_all - w), n_blocks - 1)
    lo = jnp.maximum(pick(start) - blk * bm, 0)
    hi = jnp.minimum(pick(end) - blk * bm, bm)
    prev = lambda v: jnp.concatenate([jnp.full((1,), -1, I32), v[:-1]])
    flags = (blk != prev(blk)).astype(I32) + 2 * (e != prev(e)).astype(I32)
    lanes = lambda v: jnp.pad(v.astype(F32), ((0, 0), (0, LANES - N_EXPERTS)))
    return dict(item_expert=e, item_block=blk, item_lo=lo, item_hi=hi, item_flags=flags,
                n_items=n_items.reshape(1), gseg=gseg.reshape(-1), lstart=lstart.reshape(-1),
                ngran=ngran.reshape(-1), total=end[-1:], n_xs=n_xs,
                lstart_v=jnp.broadcast_to(lanes(lstart)[:, None, :], (nt, SUBLANES, LANES)),
                gseg_v=jnp.broadcast_to(lanes(gseg)[:, None, :], (nt, SUBLANES, LANES)))


def _tail_kernel(slot_ref, slotn_ref, x1_ref, ys_ref, gate_ref, p_ref, gple_ref, wg_ref, wp_ref, gfin_ref, o_ref,
                 ybuf0, ybuf1, sem):
    i = pl.program_id(0)
    last = pl.num_programs(0) - 1
    tm = x1_ref.shape[0]

    def row_copy(rows_ref, buf, b, k, r):
        return pltpu.make_async_copy(ys_ref.at[pl.ds(rows_ref[0, 0, k * tm + r], 1)], buf.at[k, pl.ds(r, 1)],
                                     sem.at[b])

    def wait_rows(buf, b):
        for k in range(TOP_K):
            pltpu.make_async_copy(ys_ref.at[pl.ds(0, tm)], buf.at[k], sem.at[b]).wait()

    @pl.when(i == 0)
    def _():
        for k in range(TOP_K):
            lax.fori_loop(0, tm, lambda r, c, k=k: (row_copy(slot_ref, ybuf0, 0, k, r).start(), c)[1], 0, unroll=8)

    def step(cur, nxt, bc, bn):
        wait_rows(cur, bc)
        n = 0
        for k in range(TOP_K):
            for r in range(tm):
                row_copy(slotn_ref, nxt, bn, k, r).start(priority=n % 2)
                n += 1
        gates = gate_ref[...]
        x2 = x1_ref[...]
        for k in range(TOP_K):
            x2 = x2 + gates[:, k:k + 1] * cur[k]
        g = jax.nn.sigmoid(_dot(_rms(x2, gple_ref[...]).astype(BF16), wg_ref[...]))
        x3 = x2 + _dot(p_ref[...].astype(BF16), wp_ref[...]) * g
        o_ref[...] = _rms(x3, gfin_ref[...])

        @pl.when(i == last)
        def _():
            wait_rows(nxt, bn)

    @pl.when(i % 2 == 0)
    def _():
        step(ybuf0, ybuf1, 0, 1)

    @pl.when(i % 2 == 1)
    def _():
        step(ybuf1, ybuf0, 1, 0)


def _tail(slots, x1, ys, gates, p, g_ple, w_gate, w_proj, g_final, *, tm):
    t, d = x1.shape
    nt = t // tm
    rowsp = lambda w: pl.BlockSpec((tm, w), lambda i: (i, 0))
    full = lambda a: pl.BlockSpec(a.shape, lambda i: (0, 0))
    smem = lambda f: pl.BlockSpec((1, 1, TOP_K * tm), f, memory_space=pltpu.SMEM)
    return pl.pallas_call(
        _tail_kernel, grid=(nt,),
        in_specs=[smem(lambda i: (i, 0, 0)), smem(lambda i: (jnp.minimum(i + 1, nt - 1), 0, 0)), rowsp(d),
                  pl.BlockSpec(memory_space=pl.ANY), rowsp(LANES), rowsp(p.shape[1])]
        + [full(a) for a in (g_ple, w_gate, w_proj, g_final)],
        out_specs=rowsp(d), out_shape=jax.ShapeDtypeStruct((t, d), F32),
        scratch_shapes=[pltpu.VMEM((TOP_K, tm, d), F32), pltpu.VMEM((TOP_K, tm, d), F32),
                        pltpu.SemaphoreType.DMA((2,))],
        compiler_params=_params(("arbitrary",)), name="tail",
    )(slots, slots, x1, ys, gates, p, g_ple, w_gate, w_proj, g_final)


def _block_diag(a, n):
    j, _, r, c = a.shape
    eye = jnp.eye(n, dtype=a.dtype)
    return jnp.einsum("ab,jarc->jarbc", eye, a).reshape(j, n * r, n * c)


def _mixer_tail(x_tok, attn, ssm_n, p_tok, lw, *, tm, bm):
    t, d = x_tok.shape
    tm = min(tm, t)
    assert t % tm == 0
    x1, hm, idx, gates, rank, before, counts = _outproj(
        x_tok, attn, ssm_n, lw["g_attn"], lw["wo_a"], lw["wo_s"], lw["g_ffn"], lw["wr_pad"], lw["br_pad"], tm=tm)
    route = _route(counts[0, :N_EXPERTS], before[:, 0, :N_EXPERTS], t=t, bm=bm)
    xs, slot = _dispatch(route, hm, idx, rank, before, tm=tm)
    ys = _experts(route, xs, lw["w_up"], lw["b_up"], lw["w_down"], lw["b_down"], bm=bm)
    nt = t // tm
    slots = jnp.transpose(slot[:, :TOP_K].reshape(nt, tm, TOP_K), (0, 2, 1)).reshape(nt, 1, TOP_K * tm)
    return _tail(slots, x1, ys, gates, p_tok, lw["g_ple"], lw["w_ple_gate"], lw["w_ple_proj"], lw["g_final"], tm=tm)


def kernel(x_prompt, x_sample, p_prompt, p_sample, cache_k, cache_v, cache_logf, state_ssm_re, state_ssm_im,
           page_table, g_mix, w_in, b_fgate, lam_re, lam_im, log_step, b_ssm_re, b_ssm_im, c_ssm_re, c_ssm_im,
           d_ssm, w_glu, b_glu, g_attn_out, g_ssm_out, w_out, g_ffn, w_router, b_router, w_up, b_up, w_down,
           b_down, g_ple, w_ple_gate, w_ple_proj, g_final):
    depth = w_in.shape[0]
    assert depth == 1
    bp, sp, d = x_prompt.shape
    bd, sd, _ = x_sample.shape
    n_heads = b_fgate.shape[1]
    aw = n_heads * HEAD_DIM
    n_groups, n_state = lam_re.shape[1:]
    sw = n_groups * SSM_GROUP
    half = n_groups * n_state
    assert sw == SSM_LANE_BLOCKS * LANES and w_in.shape[2] == 3 * aw + n_heads + sw

    wi = w_in[0]
    wq, wk, wv = wi[:, :aw], wi[:, aw:2 * aw], wi[:, 2 * aw:3 * aw]
    wf, wu = wi[:, 3 * aw:3 * aw + n_heads], wi[:, 3 * aw + n_heads:]
    w_all = jnp.concatenate([wq, wk, wv, wu, jnp.pad(wf, ((0, 0), (0, LANES - n_heads)))], axis=1).astype(BF16)
    wft = jnp.pad(wf.T, ((0, 2 * SUBLANES - n_heads), (0, 0))).astype(BF16)
    bf_col = jnp.pad(b_fgate[0][:, None], ((0, 2 * SUBLANES - n_heads), (0, 0)))
    bf_row = jnp.pad(b_fgate, ((0, 0), (0, LANES - n_heads)))

    abar_re, abar_im, bbr_t, bbi_t = _s5_params(
        lam_re[0], lam_im[0], log_step[0], jnp.transpose(b_ssm_re[0], (0, 2, 1)), jnp.transpose(b_ssm_im[0], (0, 2, 1)))
    gpb = n_groups // SSM_LANE_BLOCKS
    blk4 = lambda a: a.reshape((SSM_LANE_BLOCKS, gpb) + a.shape[1:])
    bd_mat = jnp.concatenate([_block_diag(blk4(bbr_t), gpb), _block_diag(blk4(bbi_t), gpb)], axis=-1).astype(BF16)
    cdr = _block_diag(blk4(jnp.transpose(c_ssm_re[0], (0, 2, 1))), gpb).astype(BF16)
    cdi = _block_diag(blk4(jnp.transpose(c_ssm_im[0], (0, 2, 1))), gpb).astype(BF16)
    ssm_w = (abar_re.reshape(1, half), abar_im.reshape(1, half), bd_mat, cdr, cdi, d_ssm[0].reshape(1, sw),
             w_glu[0].astype(BF16), b_glu, g_ssm_out)

    lw = dict(g_attn=g_attn_out, wo_a=w_out[0, :aw].astype(BF16), wo_s=w_out[0, aw:].astype(BF16), g_ffn=g_ffn,
              wr_pad=jnp.pad(w_router[0], ((0, 0), (0, LANES - N_EXPERTS))).astype(BF16),
              br_pad=jnp.pad(b_router, ((0, 0), (0, LANES - N_EXPERTS))),
              w_up=w_up[0], b_up=b_up[0][:, None, :], w_down=w_down[0],
              b_down=b_down[0][:, None, :], g_ple=g_ple, w_ple_gate=w_ple_gate[0].astype(BF16),
              w_ple_proj=w_ple_proj[0].astype(BF16), g_final=g_final.reshape(1, d))

    q, k, v, kb, vb, u, lft, ct = _inproj_prompt(x_prompt, g_mix, w_all, wft, bf_col, aw=aw, n_heads=n_heads, tm=512)
    attn = _fox_prompt(q, kb, vb, ct, tq=512)
    n_time = min(64, sp)
    zeros_h = jnp.zeros((bp, half), F32)
    ssm_tb, hr_p, hi_p = _ssm(jnp.transpose(u, (1, 0, 2)).reshape(sp * bp, sw), zeros_h, zeros_h, *ssm_w,
                              n_batch=bp, n_time=n_time)
    ssm_p = jnp.transpose(ssm_tb.reshape(sp, bp, sw), (1, 0, 2)).reshape(bp * sp, sw)
    y_p = _mixer_tail(x_prompt.reshape(bp * sp, d), attn.reshape(bp * sp, aw), ssm_p,
                      p_prompt[0].reshape(bp * sp, -1), lw, tm=512, bm=512)

    ts = bd * sd
    qs, ks, vs, us, lfs = _inproj_sample(x_sample.reshape(ts, d), g_mix, w_all, bf_row, aw=aw)
    lfs = lfs[:, :n_heads]
    kt_cache = jnp.transpose(cache_k[0], (0, 2, 3, 1))
    vt_cache = jnp.transpose(cache_v[0], (0, 2, 3, 1))
    lft_cache = jnp.transpose(cache_logf[0], (0, 2, 1))
    attn_s = _fox_decode(page_table, qs.reshape(bd, sd, aw), ks.reshape(bd, sd, aw), vs.reshape(bd, sd, aw),
                         jnp.transpose(lfs.reshape(bd, sd, n_heads), (0, 2, 1)), kt_cache, vt_cache, lft_cache,
                         n_group=16)
    ssm_s_tb, hr_s, hi_s = _ssm(jnp.transpose(us.reshape(bd, sd, sw), (1, 0, 2)).reshape(ts, sw),
                                state_ssm_re[0].reshape(bd, half), state_ssm_im[0].reshape(bd, half), *ssm_w,
                                n_batch=bd, n_time=sd)
    ssm_s = jnp.transpose(ssm_s_tb.reshape(sd, bd, sw), (1, 0, 2)).reshape(ts, sw)
    y_s = _mixer_tail(x_sample.reshape(ts, d), attn_s.reshape(ts, aw), ssm_s, p_sample[0].reshape(ts, -1), lw,
                      tm=128, bm=128)

    heads = lambda a, b_, s_: a.reshape(1, b_, s_, n_heads, HEAD_DIM)
    state = lambda a, b_: a.reshape(1, b_, n_groups, n_state)
    return (y_p.reshape(bp, sp, d), y_s.reshape(bd, sd, d),
            heads(k, bp, sp), heads(v, bp, sp), jnp.transpose(lft, (0, 2, 1))[None],
            state(hr_p, bp), state(hi_p, bp),
            heads(ks, bd, sd), heads(vs, bd, sd), lfs.reshape(1, bd, sd, n_heads),
            state(hr_s, bd), state(hi_s, bd))
```

```python
import functools

import jax
import jax.numpy as jnp
from jax import lax
from jax.experimental import pallas as pl
from jax.experimental.pallas import tpu as pltpu

F32 = jnp.float32
BF16 = jnp.bfloat16
I32 = jnp.int32

RMS_EPS = 1e-6
LOG2E = 1.4426950408889634
HEAD_DIM = 64
HEADS_PER_LANE_TILE = 2
LANES = 128
SUBLANES = 8
SSM_GROUP = 16
SSM_STATE = 64
SSM_LANE_BLOCKS = 4
N_EXPERTS = 32
TOP_K = 4
GRANULE = 8
SWIGLU_ALPHA = 1.702
SWIGLU_LIMIT = 7.0
VMEM_LIMIT = 56 * 1024 * 1024


def _params(sem, vmem=VMEM_LIMIT):
    return pltpu.CompilerParams(dimension_semantics=sem, vmem_limit_bytes=vmem)


def _rms(x, g):
    return x * lax.rsqrt(jnp.mean(x * x, axis=-1, keepdims=True) + RMS_EPS) * g


def _log_sigmoid(x):
    return jnp.minimum(x, 0.0) - jnp.log1p(jnp.exp(-jnp.abs(x)))


def _split3(x):
    hi = x.astype(BF16)
    r1 = x - hi.astype(F32)
    mid = r1.astype(BF16)
    lo = (r1 - mid.astype(F32)).astype(BF16)
    return hi, mid, lo


def _dot(a, b):
    return jnp.dot(a, b, preferred_element_type=F32)


def _dot_nt(a, b):
    return lax.dot_general(a, b, (((1,), (1,)), ((), ())), preferred_element_type=F32)


def _dot3(x, m):
    hi, mid, lo = _split3(x)
    return _dot(hi, m) + _dot(mid, m) + _dot(lo, m)


def _s5_params_kernel(lr_ref, li_ref, ls_ref, brt_ref, bit_ref, ar_ref, ai_ref, bbr_ref, bbi_ref):
    lr, li = lr_ref[...], li_ref[...]
    step = jnp.exp(ls_ref[...])
    mag = jnp.exp(lr * step)
    a_re, a_im = mag * jnp.cos(li * step), mag * jnp.sin(li * step)
    nr, ni = a_re - 1.0, a_im
    den = lr * lr + li * li
    coef_re = (nr * lr + ni * li) / den
    coef_im = (ni * lr - nr * li) / den
    ar_ref[...] = a_re
    ai_ref[...] = a_im
    br, bi = brt_ref[...], bit_ref[...]
    bbr_ref[...] = coef_re * br - coef_im * bi
    bbi_ref[...] = coef_re * bi + coef_im * br


def _s5_params(lam_re, lam_im, log_step, b_re_t, b_im_t):
    g, p = lam_re.shape
    c = b_re_t.shape[1]
    return pl.pallas_call(
        _s5_params_kernel,
        out_shape=(jax.ShapeDtypeStruct((g, 1, p), F32), jax.ShapeDtypeStruct((g, 1, p), F32),
                   jax.ShapeDtypeStruct((g, c, p), F32), jax.ShapeDtypeStruct((g, c, p), F32)),
        name="s5_params",
    )(lam_re.reshape(g, 1, p), lam_im.reshape(g, 1, p), log_step.reshape(g, 1, 1), b_re_t, b_im_t)


def _inproj_core(x, g_ref, w_ref, aw):
    h = _rms(x, g_ref[...]).astype(BF16)
    z = _dot(h, w_ref[...])
    return h, z


def _inproj_prompt_kernel(x_ref, g_ref, w_ref, wft_ref, bf_ref, utri_ref,
                          q_ref, k_ref, v_ref, kb_ref, vb_ref, u_ref, lft_ref, ct_ref, carry_ref, *, aw, scale):
    tm = x_ref.shape[1]
    h, z = _inproj_core(x_ref[0], g_ref, w_ref, aw)
    q_ref[0] = (z[:, :aw] * scale).astype(BF16)
    k = z[:, aw:2 * aw]
    v = z[:, 2 * aw:3 * aw]
    k_ref[0] = k
    v_ref[0] = v
    kb_ref[0] = k.astype(BF16)
    vb_ref[0] = v.astype(BF16)
    u_ref[0] = z[:, 3 * aw:4 * aw]
    ft = _dot_nt(wft_ref[...], h) + bf_ref[...]
    lft = _log_sigmoid(ft)
    n_heads = lft_ref.shape[1]
    lft_ref[0] = lft[:n_heads]

    @pl.when(pl.program_id(1) == 0)
    def _():
        carry_ref[...] = jnp.zeros_like(carry_ref)

    cs = _dot3(lft, utri_ref[...]) + carry_ref[:, :1]
    carry_ref[...] = jnp.broadcast_to(cs[:, tm - 1:tm], carry_ref.shape)
    ct_ref[0] = cs[:n_heads]


def _inproj_prompt(x, g_mix, w_all, wft, bf_col, *, aw, n_heads, tm):
    b, s, d = x.shape
    tm = min(tm, s)
    assert s % tm == 0
    ii = lax.broadcasted_iota(I32, (tm, tm), 0)
    jj = lax.broadcasted_iota(I32, (tm, tm), 1)
    utri = (ii <= jj).astype(BF16)
    row = lambda bi, si: (bi, si, 0)
    col = lambda bi, si: (bi, 0, si)
    const = lambda bi, si: (0, 0)
    wide = pl.BlockSpec((1, tm, aw), row)
    tall = pl.BlockSpec((1, n_heads, tm), col)
    return pl.pallas_call(
        functools.partial(_inproj_prompt_kernel, aw=aw, scale=HEAD_DIM ** -0.5 * LOG2E),
        grid=(b, s // tm),
        in_specs=[pl.BlockSpec((1, tm, d), row), pl.BlockSpec((1, d), const), pl.BlockSpec(w_all.shape, const),
                  pl.BlockSpec(wft.shape, const), pl.BlockSpec(bf_col.shape, const), pl.BlockSpec((tm, tm), const)],
        out_specs=[wide, wide, wide, wide, wide, wide, tall, tall],
        out_shape=[jax.ShapeDtypeStruct((b, s, aw), BF16), jax.ShapeDtypeStruct((b, s, aw), F32),
                   jax.ShapeDtypeStruct((b, s, aw), F32), jax.ShapeDtypeStruct((b, s, aw), BF16),
                   jax.ShapeDtypeStruct((b, s, aw), BF16), jax.ShapeDtypeStruct((b, s, aw), F32),
                   jax.ShapeDtypeStruct((b, n_heads, s), F32), jax.ShapeDtypeStruct((b, n_heads, s), F32)],
        scratch_shapes=[pltpu.VMEM((2 * SUBLANES, LANES), F32)],
        compiler_params=_params(("arbitrary", "arbitrary")),
        name="inproj_prompt",
    )(x, g_mix, w_all, wft, bf_col, utri)


def _inproj_sample_kernel(x_ref, g_ref, w_ref, bf_ref, q_ref, k_ref, v_ref, u_ref, lf_ref, *, aw, scale):
    _, z = _inproj_core(x_ref[...], g_ref, w_ref, aw)
    q_ref[...] = (z[:, :aw] * scale).astype(BF16)
    k_ref[...] = z[:, aw:2 * aw]
    v_ref[...] = z[:, 2 * aw:3 * aw]
    u_ref[...] = z[:, 3 * aw:4 * aw]
    lf_ref[...] = _log_sigmoid(z[:, 4 * aw:] + bf_ref[...])


def _inproj_sample(x, g_mix, w_all, bf_row, *, aw):
    t, d = x.shape
    outs = [jax.ShapeDtypeStruct((t, aw), BF16)] + [jax.ShapeDtypeStruct((t, aw), F32)] * 3
    outs.append(jax.ShapeDtypeStruct((t, LANES), F32))
    return pl.pallas_call(
        functools.partial(_inproj_sample_kernel, aw=aw, scale=HEAD_DIM ** -0.5),
        out_shape=outs, compiler_params=_params(None), name="inproj_sample",
    )(x, g_mix, w_all, bf_row)


_L_ONE, _L_BIAS = HEAD_DIM, HEAD_DIM + 3


def _bias_rows(c_row, piece_row0, ones_row0, n_ones, width):
    hi, mid, lo = (p.astype(F32) for p in _split3(c_row))
    r = lax.broadcasted_iota(I32, (LANES, width), 0)
    out = jnp.where((r >= ones_row0) & (r < ones_row0 + n_ones), 1.0, 0.0)
    for i, piece in enumerate((hi, mid, lo)):
        out = jnp.where(r == piece_row0 + i, piece, out)
    return out


def _fox_prompt_kernel(q_ref, k_ref, v_ref, ct_ref, o_ref, kaug, vaug, *, tq):
    qi = pl.program_id(2)
    pair = pl.program_id(1)
    n_kv = k_ref.shape[1] // tq
    lane = lax.broadcasted_iota(I32, (tq, LANES), 1)
    row = lax.broadcasted_iota(I32, (tq, tq), 0)
    colm = lax.broadcasted_iota(I32, (tq, tq), 1)
    src = lax.broadcasted_iota(I32, (LANES, LANES), 0)
    dst = lax.broadcasted_iota(I32, (LANES, LANES), 1)
    sels = [((src == dst + hh * HEAD_DIM) & (dst < HEAD_DIM)).astype(BF16) for hh in range(HEADS_PER_LANE_TILE)]

    @pl.when(qi == 0)
    def _():
        for hh in range(HEADS_PER_LANE_TILE):
            head = pair * HEADS_PER_LANE_TILE + hh
            for c in range(n_kv):
                rows = slice(c * tq, (c + 1) * tq)
                ck = ct_ref[0, head, c:c + 1, :] * LOG2E
                kb = _bias_rows(-ck, _L_ONE, _L_BIAS, 3, tq).T
                kaug[hh, rows, :] = (_dot(k_ref[0, rows, :], sels[hh]) + kb).astype(BF16)
                vaug[hh, rows, :] = (_dot(v_ref[0, rows, :], sels[hh])
                                     + jnp.where(lane == _L_ONE, 1.0, 0.0)).astype(BF16)

    qas = []
    for hh in range(HEADS_PER_LANE_TILE):
        cq = ct_ref[0, pair * HEADS_PER_LANE_TILE + hh, pl.ds(qi, 1), :] * LOG2E
        qas.append((_dot(q_ref[0], sels[hh]) + _bias_rows(cq, _L_BIAS, _L_ONE, 3, tq).T).astype(BF16))

    def tile(j, carry, masked):
        start = pl.multiple_of(j * tq, tq)
        new = []
        for hh in range(HEADS_PER_LANE_TILE):
            m, acc = carry[hh]
            ka = kaug[hh, pl.ds(start, tq), :]
            va = vaug[hh, pl.ds(start, tq), :]
            s = _dot_nt(qas[hh], ka)
            if masked:
                s = jnp.where(colm <= row, s, -jnp.inf)
            m_new = jnp.maximum(m, jnp.max(s, axis=-1, keepdims=True))
            p = jnp.exp2(s - m_new).astype(BF16)
            acc = jnp.exp2(m - m_new) * acc + _dot(p, va)
            new.append((m_new, acc))
        return tuple(new)

    init = tuple((jnp.full((tq, 1), -jnp.inf, F32), jnp.zeros((tq, LANES), F32))
                 for _ in range(HEADS_PER_LANE_TILE))
    carry = lax.fori_loop(0, qi // 2, lambda jj, c: tile(2 * jj + 1, tile(2 * jj, c, False), False), init)
    carry = lax.cond(qi % 2 == 1, lambda c: tile(qi - 1, c, False), lambda c: c, carry)
    (_, acc0), (_, acc1) = tile(qi, carry, True)
    out0 = acc0 / acc0[:, _L_ONE:_L_ONE + 1]
    out1 = acc1 / acc1[:, _L_ONE:_L_ONE + 1]
    o_ref[0] = jnp.where(lane < HEAD_DIM, out0, pltpu.roll(out1, HEAD_DIM, 1))


def _fox_prompt(q, kb, vb, ct, *, tq):
    b, s, aw = q.shape
    tq = min(tq, s)
    assert s % tq == 0
    n_pairs = aw // LANES
    ct = ct.reshape(b, ct.shape[1], s // tq, tq)
    return pl.pallas_call(
        functools.partial(_fox_prompt_kernel, tq=tq),
        grid=(b, n_pairs, s // tq),
        in_specs=[pl.BlockSpec((1, tq, LANES), lambda bi, p, qi: (bi, qi, p)),
                  pl.BlockSpec((1, s, LANES), lambda bi, p, qi: (bi, 0, p)),
                  pl.BlockSpec((1, s, LANES), lambda bi, p, qi: (bi, 0, p)),
                  pl.BlockSpec((1,) + ct.shape[1:], lambda bi, p, qi: (bi, 0, 0, 0))],
        out_specs=pl.BlockSpec((1, tq, LANES), lambda bi, p, qi: (bi, qi, p)),
        out_shape=jax.ShapeDtypeStruct((b, s, aw), F32),
        scratch_shapes=[pltpu.VMEM((HEADS_PER_LANE_TILE, s, LANES), BF16),
                        pltpu.VMEM((HEADS_PER_LANE_TILE, s, LANES), BF16)],
        compiler_params=_params(("arbitrary", "arbitrary", "arbitrary")),
        name="fox_prompt",
    )(q, kb, vb, ct)


def _fox_decode_kernel(pt_ref, q_ref, kn_ref, vn_ref, lfn_ref, ustrict_ref, *rest, n_group, n_new):
    kt_refs = rest[:n_group]
    vt_refs = rest[n_group:2 * n_group]
    lf_refs = rest[2 * n_group:3 * n_group]
    o_ref, m_ref, l_ref, acc_ref, tail_ref = rest[3 * n_group:]
    j = pl.program_id(1)
    n_heads = lfn_ref.shape[1]
    width = q_ref.shape[2]
    rows = n_new * n_heads

    lane_head = lax.broadcasted_iota(I32, (n_heads, width), 1) // HEAD_DIM
    sub = lax.broadcasted_iota(I32, (n_heads, width), 0)
    hmask = lane_head == sub
    q = q_ref[0].astype(F32)
    qexp = jnp.concatenate([jnp.where(hmask, jnp.broadcast_to(q[t:t + 1], (n_heads, width)), 0.0)
                            for t in range(n_new)], axis=0)
    trow = lax.broadcasted_iota(I32, (rows, 1), 0) // n_heads

    @pl.when(j == 0)
    def _():
        lfn = lfn_ref[0]
        kn, vn = kn_ref[0], vn_ref[0]
        run = jnp.zeros((n_heads, 1), F32)
        logits = []
        for sp in range(n_new):
            run = run + lfn[:, sp:sp + 1]
            bias = jnp.concatenate([run] * n_new, axis=0)
            sc = jnp.sum(qexp * kn[sp:sp + 1], axis=-1, keepdims=True) - bias
            logits.append(jnp.where(trow >= sp, sc, -jnp.inf))
        m = logits[0]
        for sc in logits[1:]:
            m = jnp.maximum(m, sc)
        l = jnp.zeros((rows, 1), F32)
        acc = jnp.zeros((rows, width), F32)
        for sp in range(n_new):
            p = jnp.exp(logits[sp] - m)
            l = l + p
            acc = acc + p * vn[sp:sp + 1]
        m_ref[...] = m
        l_ref[...] = l
        acc_ref[...] = acc
        tail_ref[...] = jnp.zeros_like(tail_ref)

    qb = qexp.astype(BF16)
    tail = tail_ref[:, :1]
    scores = []
    for g in range(n_group):
        kt = kt_refs[g][0].reshape(width, -1).astype(BF16)
        lf = lf_refs[g][0]
        lf_pad = jnp.concatenate([lf, jnp.zeros_like(lf)], axis=0)
        suffix = _dot3(lf_pad, ustrict_ref[...])[:n_heads] + tail
        tail = tail + jnp.sum(lf, axis=-1, keepdims=True)
        scores.append(_dot(qb, kt) + jnp.concatenate([suffix] * n_new, axis=0))
    tail_ref[...] = jnp.broadcast_to(tail, tail_ref.shape)
    m_old = m_ref[...]
    m_new = m_old
    for s in scores:
        m_new = jnp.maximum(m_new, jnp.max(s, axis=-1, keepdims=True))
    alpha = jnp.exp(m_old - m_new)
    l = alpha * l_ref[...]
    acc = alpha * acc_ref[...]
    for g, s in enumerate(scores):
        p = jnp.exp(s - m_new)
        l = l + jnp.sum(p, axis=-1, keepdims=True)
        acc = acc + _dot_nt(p.astype(BF16), vt_refs[g][0].reshape(width, -1).astype(BF16))
    l_ref[...] = l
    acc_ref[...] = acc
    m_ref[...] = m_new

    @pl.when(j == pl.num_programs(1) - 1)
    def _():
        o = acc_ref[...] / l_ref[...]
        outs = [jnp.sum(jnp.where(hmask, o[t * n_heads:(t + 1) * n_heads], 0.0), axis=0, keepdims=True)
                for t in range(n_new)]
        o_ref[0] = jnp.concatenate(outs, axis=0)


def _fox_decode(page_table, q, k_new, v_new, lfn_t, kt_cache, vt_cache, lft_cache, *, n_group):
    bd, n_new, width = q.shape
    n_pages = page_table.shape[1]
    n_heads, page = lft_cache.shape[1:]
    n_group = min(n_group, n_pages)
    assert n_pages % n_group == 0
    n_steps = n_pages // n_group
    ii = lax.broadcasted_iota(I32, (page, page), 0)
    jj = lax.broadcasted_iota(I32, (page, page), 1)
    ustrict = (ii > jj).astype(BF16)

    def page_map(g, ndim):
        def index_map(b, j, pt):
            return (pt[b, n_pages - 1 - (j * n_group + g)],) + (0,) * ndim
        return index_map

    per_batch = lambda b, j, pt: (b, 0, 0)
    in_specs = [pl.BlockSpec((1, n_new, width), per_batch), pl.BlockSpec((1, n_new, width), per_batch),
                pl.BlockSpec((1, n_new, width), per_batch), pl.BlockSpec((1, n_heads, n_new), per_batch),
                pl.BlockSpec((page, page), lambda b, j, pt: (0, 0))]
    in_specs += [pl.BlockSpec((1,) + kt_cache.shape[1:], page_map(g, 3)) for g in range(n_group)]
    in_specs += [pl.BlockSpec((1,) + vt_cache.shape[1:], page_map(g, 3)) for g in range(n_group)]
    in_specs += [pl.BlockSpec((1, n_heads, page), page_map(g, 2)) for g in range(n_group)]
    rows = n_new * n_heads
    grid_spec = pltpu.PrefetchScalarGridSpec(
        num_scalar_prefetch=1, grid=(bd, n_steps), in_specs=in_specs,
        out_specs=pl.BlockSpec((1, n_new, width), per_batch),
        scratch_shapes=[pltpu.VMEM((rows, 1), F32), pltpu.VMEM((rows, 1), F32), pltpu.VMEM((rows, width), F32),
                        pltpu.VMEM((n_heads, LANES), F32)])
    return pl.pallas_call(
        functools.partial(_fox_decode_kernel, n_group=n_group, n_new=n_new),
        grid_spec=grid_spec, out_shape=jax.ShapeDtypeStruct((bd, n_new, width), F32),
        compiler_params=_params(("arbitrary", "arbitrary")), name="fox_decode",
    )(page_table, q, k_new, v_new, lfn_t, ustrict,
      *([kt_cache] * n_group), *([vt_cache] * n_group), *([lft_cache] * n_group))


def _ssm_kernel(u_ref, h0r_ref, h0i_ref, ar_ref, ai_ref, bd_ref, cdr_ref, cdi_ref, d_ref, wglu_ref, bglu_ref,
                gout_ref, o_ref, hr_ref, hi_ref, bu_ref, h_ref, *, n_batch, n_time):
    i = pl.program_id(0)
    half = ar_ref.shape[1]
    blk = half // SSM_LANE_BLOCKS
    rows = n_batch * n_time

    @pl.when(i == 0)
    def _():
        h_ref[:, :half] = h0r_ref[...]
        h_ref[:, half:] = h0i_ref[...]

    u = u_ref[...]
    ub = u.astype(BF16)
    for jb in range(SSM_LANE_BLOCKS):
        bu = _dot(ub[:, jb * LANES:(jb + 1) * LANES], bd_ref[jb])
        bu_ref[:, jb * blk:(jb + 1) * blk] = bu[:, :blk]
        bu_ref[:, half + jb * blk:half + (jb + 1) * blk] = bu[:, blk:]

    chunk = blk
    for bg in range(n_batch // SUBLANES):
        for c in range(half // chunk):
            lo_r, lo_i = c * chunk, half + c * chunk
            a_r = jnp.broadcast_to(ar_ref[:, lo_r:lo_r + chunk], (SUBLANES, chunk))
            a_i = jnp.broadcast_to(ai_ref[:, lo_r:lo_r + chunk], (SUBLANES, chunk))
            h_r0 = h_ref[bg * SUBLANES:(bg + 1) * SUBLANES, lo_r:lo_r + chunk]
            h_i0 = h_ref[bg * SUBLANES:(bg + 1) * SUBLANES, lo_i:lo_i + chunk]

            def step(t, carry, lo_r=lo_r, lo_i=lo_i, a_r=a_r, a_i=a_i, bg=bg):
                h_r, h_i = carry
                r0 = pl.multiple_of(t * n_batch + bg * SUBLANES, SUBLANES)
                n_r = a_r * h_r - a_i * h_i + bu_ref[pl.ds(r0, SUBLANES), lo_r:lo_r + chunk]
                n_i = a_r * h_i + a_i * h_r + bu_ref[pl.ds(r0, SUBLANES), lo_i:lo_i + chunk]
                bu_ref[pl.ds(r0, SUBLANES), lo_r:lo_r + chunk] = n_r
                bu_ref[pl.ds(r0, SUBLANES), lo_i:lo_i + chunk] = n_i
                return n_r, n_i

            h_r, h_i = lax.fori_loop(0, n_time, step, (h_r0, h_i0), unroll=min(4, n_time))
            h_ref[bg * SUBLANES:(bg + 1) * SUBLANES, lo_r:lo_r + chunk] = h_r
            h_ref[bg * SUBLANES:(bg + 1) * SUBLANES, lo_i:lo_i + chunk] = h_i

    ys = []
    for jb in range(SSM_LANE_BLOCKS):
        hre = bu_ref[:, jb * blk:(jb + 1) * blk].astype(BF16)
        him = bu_ref[:, half + jb * blk:half + (jb + 1) * blk].astype(BF16)
        ys.append(_dot(hre, cdr_ref[jb]) - _dot(him, cdi_ref[jb]))
    y = jnp.concatenate(ys, axis=-1) + d_ref[...] * u
    y = jax.nn.gelu(y, approximate=True)
    y = y * jax.nn.sigmoid(_dot(y.astype(BF16), wglu_ref[...]) + bglu_ref[...])
    o_ref[...] = _rms(y, gout_ref[...]).astype(BF16)

    @pl.when(i == pl.num_programs(0) - 1)
    def _():
        hr_ref[...] = h_ref[:, :half]
        hi_ref[...] = h_ref[:, half:]


def _ssm(u_tb, h0r, h0i, abar_re, abar_im, bd, cdr, cdi, d_row, w_glu, b_glu, g_out, *, n_batch, n_time):
    total, width = u_tb.shape
    half = abar_re.shape[1]
    rows = n_batch * n_time
    assert total % rows == 0 and n_batch % SUBLANES == 0
    const2 = lambda i: (0, 0)
    const3 = lambda i: (0, 0, 0)
    full = lambda a: pl.BlockSpec(a.shape, const2 if a.ndim == 2 else const3)
    return pl.pallas_call(
        functools.partial(_ssm_kernel, n_batch=n_batch, n_time=n_time),
        grid=(total // rows,),
        in_specs=[pl.BlockSpec((rows, width), lambda i: (i, 0))] + [full(a) for a in (
            h0r, h0i, abar_re, abar_im, bd, cdr, cdi, d_row, w_glu, b_glu, g_out)],
        out_specs=[pl.BlockSpec((rows, width), lambda i: (i, 0)), pl.BlockSpec((n_batch, half), const2),
                   pl.BlockSpec((n_batch, half), const2)],
        out_shape=[jax.ShapeDtypeStruct((total, width), BF16), jax.ShapeDtypeStruct((n_batch, half), F32),
                   jax.ShapeDtypeStruct((n_batch, half), F32)],
        scratch_shapes=[pltpu.VMEM((rows, 2 * half), F32), pltpu.VMEM((n_batch, 2 * half), F32)],
        compiler_params=_params(("arbitrary",)), name="ssm",
    )(u_tb, h0r, h0i, abar_re, abar_im, bd, cdr, cdi, d_row, w_glu, b_glu, g_out)


def _outproj_kernel(x_ref, a_ref, s_ref, ga_ref, woa_ref, wos_ref, gffn_ref, wr_ref, br_ref, lstrict_ref,
                    x1_ref, hm_ref, idx_ref, gate_ref, rank_ref, before_ref, cnt_ref, cnt_sc):
    i = pl.program_id(0)
    tm = x_ref.shape[0]

    @pl.when(i == 0)
    def _():
        cnt_sc[...] = jnp.zeros_like(cnt_sc)

    before_ref[0] = cnt_sc[...]

    an = _rms(a_ref[...], ga_ref[...]).astype(BF16)
    x1 = x_ref[...] + _dot(an, woa_ref[...]) + _dot(s_ref[...], wos_ref[...])
    x1_ref[...] = x1
    hm = _rms(x1, gffn_ref[...])
    hm_ref[...] = hm
    logits = _dot(hm.astype(BF16), wr_ref[...]) + br_ref[...]
    lane_i = lax.broadcasted_iota(I32, (tm, LANES), 1)
    lane = lane_i.astype(F32)
    work = jnp.where(lane_i < N_EXPERTS, logits, -jnp.inf)
    vals, idxs = [], []
    for _ in range(TOP_K):
        mx = jnp.max(work, axis=-1, keepdims=True)
        ix = jnp.min(jnp.where(work == mx, lane, float(LANES)), axis=-1, keepdims=True)
        vals.append(mx)
        idxs.append(ix)
        work = jnp.where(lane == ix, -jnp.inf, work)
    exps = [jnp.exp(v - vals[0]) for v in vals]
    denom = exps[0]
    for e in exps[1:]:
        denom = denom + e
    onehot = jnp.zeros((tm, LANES), F32)
    for ix in idxs:
        onehot = onehot + (lane == ix).astype(F32)
    before = _dot(lstrict_ref[...], onehot.astype(BF16)) + cnt_sc[:1, :]
    idx_out = jnp.zeros((tm, LANES), I32)
    gate_out = jnp.zeros((tm, LANES), F32)
    rank_out = jnp.zeros((tm, LANES), I32)
    for k in range(TOP_K):
        rk = jnp.sum(jnp.where(lane == idxs[k], before, 0.0), axis=-1, keepdims=True)
        idx_out = jnp.where(lane_i == k, idxs[k].astype(I32), idx_out)
        gate_out = jnp.where(lane_i == k, exps[k] / denom, gate_out)
        rank_out = jnp.where(lane_i == k, rk.astype(I32), rank_out)
    idx_ref[...] = idx_out
    gate_ref[...] = gate_out
    rank_ref[...] = rank_out
    cnt = cnt_sc[:1, :] + jnp.sum(onehot, axis=0, keepdims=True)
    cnt_sc[...] = jnp.broadcast_to(cnt, cnt_sc.shape)
    cnt_ref[...] = jnp.broadcast_to(cnt, cnt_ref.shape)


def _outproj(x, attn, ssm_n, g_attn, wo_a, wo_s, g_ffn, wr_pad, br_pad, *, tm):
    t, d = x.shape
    tm = min(tm, t)
    assert t % tm == 0
    ii = lax.broadcasted_iota(I32, (tm, tm), 0)
    jj = lax.broadcasted_iota(I32, (tm, tm), 1)
    lstrict = (jj < ii).astype(BF16)
    rowsp = lambda w: pl.BlockSpec((tm, w), lambda i: (i, 0))
    full = lambda a: pl.BlockSpec(a.shape, lambda i: (0, 0))
    aw = attn.shape[1]
    return pl.pallas_call(
        _outproj_kernel, grid=(t // tm,),
        in_specs=[rowsp(d), rowsp(aw), rowsp(ssm_n.shape[1])] + [full(a) for a in (
            g_attn, wo_a, wo_s, g_ffn, wr_pad, br_pad, lstrict)],
        out_specs=[rowsp(d), rowsp(d), rowsp(LANES), rowsp(LANES), rowsp(LANES),
                   pl.BlockSpec((1, SUBLANES, LANES), lambda i: (i, 0, 0)),
                   pl.BlockSpec((SUBLANES, LANES), lambda i: (0, 0))],
        out_shape=[jax.ShapeDtypeStruct((t, d), F32), jax.ShapeDtypeStruct((t, d), F32),
                   jax.ShapeDtypeStruct((t, LANES), I32), jax.ShapeDtypeStruct((t, LANES), F32),
                   jax.ShapeDtypeStruct((t, LANES), I32), jax.ShapeDtypeStruct((t // tm, SUBLANES, LANES), F32),
                   jax.ShapeDtypeStruct((SUBLANES, LANES), F32)],
        scratch_shapes=[pltpu.VMEM((SUBLANES, LANES), F32)],
        compiler_params=_params(("arbitrary",)), name="outproj_router",
    )(x, attn, ssm_n, g_attn, wo_a, wo_s, g_ffn, wr_pad, br_pad, lstrict)


def _dispatch_kernel(gseg_ref, lstart_ref, ngran_ref, total_ref, hm_ref, idx_ref, rank_ref, before_ref, lstartv_ref,
                     gsegv_ref, xs_ref, slot_ref, sorted_ref, zeros_ref, sem, *, chunk):
    i = pl.program_id(0)
    tm = hm_ref.shape[0]
    lane = lax.broadcasted_iota(I32, (tm, LANES), 1)
    idx = idx_ref[...]
    rank = rank_ref[...].astype(F32)
    local_off = lstartv_ref[0][:1] - before_ref[0][:1]
    global_off = gsegv_ref[0][:1] - before_ref[0][:1]
    lslots = jnp.full((tm, LANES), -1.0, F32)
    gslots = jnp.zeros((tm, LANES), I32)
    for k in range(TOP_K):
        sel = lane == idx[:, k:k + 1]
        r_k = rank[:, k:k + 1]
        ls = jnp.sum(jnp.where(sel, local_off, 0.0), axis=-1, keepdims=True) + r_k
        gs = jnp.sum(jnp.where(sel, global_off, 0.0), axis=-1, keepdims=True) + r_k
        lslots = jnp.where(lane == k, ls, lslots)
        gslots = jnp.where(lane == k, gs.astype(I32), gslots)
    slot_ref[...] = gslots
    lt = lslots.T
    hb = hm_ref[...].astype(BF16)
    for c in range(sorted_ref.shape[0] // chunk):
        pos = (c * chunk + lax.broadcasted_iota(I32, (chunk, tm), 0)).astype(F32)
        perm = jnp.zeros((chunk, tm), F32)
        for k in range(TOP_K):
            perm = perm + jnp.where(pos == lt[k:k + 1], 1.0, 0.0)
        sorted_ref[c * chunk:(c + 1) * chunk, :] = _dot(perm.astype(BF16), hb)

    def copy(src, dst):
        return pltpu.make_async_copy(sorted_ref.at[pl.ds(src, GRANULE)], xs_ref.at[pl.ds(dst, GRANULE)], sem.at[0])

    total = 0
    for e in range(N_EXPERTS):
        n = ngran_ref[i * N_EXPERTS + e]
        src0 = lstart_ref[i * N_EXPERTS + e]
        dst0 = gseg_ref[i * N_EXPERTS + e]

        def body(g, c, src0=src0, dst0=dst0):
            copy(pl.multiple_of(src0 + g * GRANULE, GRANULE), pl.multiple_of(dst0 + g * GRANULE, GRANULE)).start()
            return c
        lax.fori_loop(0, n, body, 0)
        total = total + n

    def wait_one(g, c):
        copy(0, 0).wait()
        return c
    lax.fori_loop(0, total, wait_one, 0)

    @pl.when(i == pl.num_programs(0) - 1)
    def _():
        zeros_ref[...] = jnp.zeros_like(zeros_ref)
        end = total_ref[0]

        def fill(g):
            dst = pl.multiple_of(end + g * GRANULE, GRANULE)
            return pltpu.make_async_copy(zeros_ref, xs_ref.at[pl.ds(dst, GRANULE)], sem.at[0])

        n_fill = (xs_ref.shape[0] - end) // GRANULE
        lax.fori_loop(0, n_fill, lambda g, c: (fill(g).start(), c)[1], 0)
        lax.fori_loop(0, n_fill, lambda g, c: (fill(g).wait(), c)[1], 0)


def _dispatch(route, hm, idx, rank, before, *, tm):
    t, d = hm.shape
    nt = t // tm
    n_local = TOP_K * tm + N_EXPERTS * GRANULE
    chunk = 256 if n_local % 256 == 0 else LANES
    assert n_local % chunk == 0
    rowsp = lambda w: pl.BlockSpec((tm, w), lambda i, *_: (i, 0))
    tile3 = pl.BlockSpec((1, SUBLANES, LANES), lambda i, *_: (i, 0, 0))
    grid_spec = pltpu.PrefetchScalarGridSpec(
        num_scalar_prefetch=4, grid=(nt,),
        in_specs=[rowsp(d), rowsp(LANES), rowsp(LANES), tile3, tile3, tile3],
        out_specs=[pl.BlockSpec(memory_space=pl.ANY), rowsp(LANES)],
        scratch_shapes=[pltpu.VMEM((n_local, d), F32), pltpu.VMEM((GRANULE, d), F32),
                        pltpu.SemaphoreType.DMA((1,))])
    return pl.pallas_call(
        functools.partial(_dispatch_kernel, chunk=chunk),
        grid_spec=grid_spec,
        out_shape=[jax.ShapeDtypeStruct((route["n_xs"], d), F32), jax.ShapeDtypeStruct((t, LANES), I32)],
        compiler_params=_params(("arbitrary",)), name="dispatch",
    )(route["gseg"], route["lstart"], route["ngran"], route["total"], hm, idx, rank, before,
      route["lstart_v"], route["gseg_v"])


def _experts_kernel(ie_ref, ib_ref, lo_ref, hi_ref, fl_ref, ni_ref, xs_ref, wup_ref, bup_ref, wdn_ref, bdn_ref,
                    ys_ref, wup_b, wdn_b, *, d_ff):
    w = pl.program_id(0)
    bm = xs_ref.shape[0]

    @pl.when(w >= ni_ref[0])
    def _():
        ys_ref[...] = jnp.zeros_like(ys_ref)

    @pl.when(w < ni_ref[0])
    def _():
        first = (fl_ref[w] & 1) == 1

        @pl.when((fl_ref[w] & 2) == 2)
        def _():
            step = 128
            for r in range(0, wup_b.shape[0], step):
                wup_b[r:r + step, :] = wup_ref[0, r:r + step, :].astype(BF16)
            for r in range(0, wdn_b.shape[0], step):
                wdn_b[r:r + step, :] = wdn_ref[0, r:r + step, :].astype(BF16)

        x = xs_ref[...].astype(BF16)
        up = _dot(x, wup_b[...]) + bup_ref[0]
        gate = jnp.minimum(up[:, :d_ff], SWIGLU_LIMIT)
        lin = jnp.clip(up[:, d_ff:], -SWIGLU_LIMIT, SWIGLU_LIMIT)
        act = gate * jax.nn.sigmoid(SWIGLU_ALPHA * gate) * (lin + 1.0)
        out = _dot(act.astype(BF16), wdn_b[...]) + bdn_ref[0]

        @pl.when(first)
        def _():
            ys_ref[...] = out

        @pl.when(jnp.logical_not(first))
        def _():
            rowi = lax.broadcasted_iota(I32, (bm, 1), 0)
            mine = (rowi >= lo_ref[w]) & (rowi < hi_ref[w])
            ys_ref[...] = jnp.where(mine, out, ys_ref[...])


def _experts(route, xs, w_up, b_up, w_down, b_down, *, bm):
    n_xs, d = xs.shape
    n_work = route["item_expert"].shape[0]
    d_ff = w_down.shape[1]
    by_block = lambda w, ie, ib, lo, hi, fl, ni: (ib[w], 0)
    by_expert = lambda w, ie, ib, lo, hi, fl, ni: (ie[w], 0, 0)
    grid_spec = pltpu.PrefetchScalarGridSpec(
        num_scalar_prefetch=6, grid=(n_work,),
        in_specs=[pl.BlockSpec((bm, d), by_block),
                  pl.BlockSpec((1,) + w_up.shape[1:], by_expert), pl.BlockSpec((1, 1, b_up.shape[2]), by_expert),
                  pl.BlockSpec((1,) + w_down.shape[1:], by_expert), pl.BlockSpec((1, 1, d), by_expert)],
        out_specs=pl.BlockSpec((bm, d), by_block),
        scratch_shapes=[pltpu.VMEM(w_up.shape[1:], BF16), pltpu.VMEM(w_down.shape[1:], BF16)])
    return pl.pallas_call(
        functools.partial(_experts_kernel, d_ff=d_ff),
        grid_spec=grid_spec, out_shape=jax.ShapeDtypeStruct((n_xs, d), F32),
        compiler_params=_params(("arbitrary",)), name="experts",
    )(route["item_expert"], route["item_block"], route["item_lo"], route["item_hi"], route["item_flags"],
      route["n_items"], xs, w_up, b_up, w_down, b_down)


def _route(counts, before, *, t, bm):
    experts = jnp.arange(N_EXPERTS, dtype=I32)
    cnt = counts.astype(I32)
    bef = before.astype(I32)
    nt = bef.shape[0]
    local = jnp.concatenate([bef[1:], cnt[None]], axis=0) - bef
    ngran = (local + GRANULE - 1) // GRANULE
    lpad = ngran * GRANULE
    lend = jnp.cumsum(lpad, axis=1)
    lstart = lend - lpad
    tot = jnp.sum(lpad, axis=0)
    end = jnp.cumsum(tot)
    start = end - tot
    gseg = start[None] + jnp.cumsum(lpad, axis=0) - lpad
    max_rows = t * TOP_K + nt * N_EXPERTS * (GRANULE - 1)
    n_blocks = -(-max_rows // bm) + 1
    n_xs = n_blocks * bm
    n_work = n_blocks + N_EXPERTS - 1
    first_blk = start // bm
    n_touch = jnp.where(tot > 0, (end - 1) // bm - first_blk + 1, 0)
    wend = jnp.cumsum(n_touch)
    wstart = wend - n_touch
    n_items = wend[-1]
    w_all = jnp.arange(n_work, dtype=I32)
    w = jnp.minimum(w_all, n_items - 1)
    e = jnp.minimum(jnp.sum((w[:, None] >= wend[None, :]).astype(I32), axis=1), N_EXPERTS - 1)
    onehot = (e[:, None] == experts[None, :]).astype(I32)
    pick = lambda v: jnp.sum(onehot * v[None, :], axis=1)
    blk = pick(first_blk) + (w - pick(wstart))
    blk = jnp.minimum(blk + (w_all - w), n_blocks - 1)
    lo = jnp.maximum(pick(start) - blk * bm, 0)
    hi = jnp.minimum(pick(end) - blk * bm, bm)
    prev = lambda v: jnp.concatenate([jnp.full((1,), -1, I32), v[:-1]])
    flags = (blk != prev(blk)).astype(I32) + 2 * (e != prev(e)).astype(I32)
    lanes = lambda v: jnp.pad(v.astype(F32), ((0, 0), (0, LANES - N_EXPERTS)))
    return dict(item_expert=e, item_block=blk, item_lo=lo, item_hi=hi, item_flags=flags,
                n_items=n_items.reshape(1), gseg=gseg.reshape(-1), lstart=lstart.reshape(-1),
                ngran=ngran.reshape(-1), total=end[-1:], n_xs=n_xs,
                lstart_v=jnp.broadcast_to(lanes(lstart)[:, None, :], (nt, SUBLANES, LANES)),
                gseg_v=jnp.broadcast_to(lanes(gseg)[:, None, :], (nt, SUBLANES, LANES)))


def _tail_kernel(slot_ref, slotn_ref, x1_ref, ys_ref, gate_ref, p_ref, gple_ref, wg_ref, wp_ref, gfin_ref, o_ref,
                 ybuf0, ybuf1, sem):
    i = pl.program_id(0)
    last = pl.num_programs(0) - 1
    tm = x1_ref.shape[0]

    def row_copy(rows_ref, buf, b, k, r):
        return pltpu.make_async_copy(ys_ref.at[pl.ds(rows_ref[0, 0, k * tm + r], 1)], buf.at[k, pl.ds(r, 1)],
                                     sem.at[b])

    def wait_rows(buf, b):
        for k in range(TOP_K):
            pltpu.make_async_copy(ys_ref.at[pl.ds(0, tm)], buf.at[k], sem.at[b]).wait()

    @pl.when(i == 0)
    def _():
        for k in range(TOP_K):
            lax.fori_loop(0, tm, lambda r, c, k=k: (row_copy(slot_ref, ybuf0, 0, k, r).start(), c)[1], 0, unroll=8)

    def step(cur, nxt, bc, bn):
        wait_rows(cur, bc)
        n = 0
        for k in range(TOP_K):
            for r in range(tm):
                row_copy(slotn_ref, nxt, bn, k, r).start(priority=n % 2)
                n += 1
        gates = gate_ref[...]
        x2 = x1_ref[...]
        for k in range(TOP_K):
            x2 = x2 + gates[:, k:k + 1] * cur[k]
        g = jax.nn.sigmoid(_dot(_rms(x2, gple_ref[...]).astype(BF16), wg_ref[...]))
        x3 = x2 + _dot(p_ref[...].astype(BF16), wp_ref[...]) * g
        o_ref[...] = _rms(x3, gfin_ref[...])

        @pl.when(i == last)
        def _():
            wait_rows(nxt, bn)

    @pl.when(i % 2 == 0)
    def _():
        step(ybuf0, ybuf1, 0, 1)

    @pl.when(i % 2 == 1)
    def _():
        step(ybuf1, ybuf0, 1, 0)


def _tail(slots, x1, ys, gates, p, g_ple, w_gate, w_proj, g_final, *, tm):
    t, d = x1.shape
    nt = t // tm
    rowsp = lambda w: pl.BlockSpec((tm, w), lambda i: (i, 0))
    full = lambda a: pl.BlockSpec(a.shape, lambda i: (0, 0))
    smem = lambda f: pl.BlockSpec((1, 1, TOP_K * tm), f, memory_space=pltpu.SMEM)
    return pl.pallas_call(
        _tail_kernel, grid=(nt,),
        in_specs=[smem(lambda i: (i, 0, 0)), smem(lambda i: (jnp.minimum(i + 1, nt - 1), 0, 0)), rowsp(d),
                  pl.BlockSpec(memory_space=pl.ANY), rowsp(LANES), rowsp(p.shape[1])]
        + [full(a) for a in (g_ple, w_gate, w_proj, g_final)],
        out_specs=rowsp(d), out_shape=jax.ShapeDtypeStruct((t, d), F32),
        scratch_shapes=[pltpu.VMEM((TOP_K, tm, d), F32), pltpu.VMEM((TOP_K, tm, d), F32),
                        pltpu.SemaphoreType.DMA((2,))],
        compiler_params=_params(("arbitrary",)), name="tail",
    )(slots, slots, x1, ys, gates, p, g_ple, w_gate, w_proj, g_final)


def _block_diag(a, n):
    j, _, r, c = a.shape
    eye = jnp.eye(n, dtype=a.dtype)
    return jnp.einsum("ab,jarc->jarbc", eye, a).reshape(j, n * r, n * c)


def _mixer_tail(x_tok, attn, ssm_n, p_tok, lw, *, tm, bm):
    t, d = x_tok.shape
    tm = min(tm, t)
    assert t % tm == 0
    x1, hm, idx, gates, rank, before, counts = _outproj(
        x_tok, attn, ssm_n, lw["g_attn"], lw["wo_a"], lw["wo_s"], lw["g_ffn"], lw["wr_pad"], lw["br_pad"], tm=tm)
    route = _route(counts[0, :N_EXPERTS], before[:, 0, :N_EXPERTS], t=t, bm=bm)
    xs, slot = _dispatch(route, hm, idx, rank, before, tm=tm)
    ys = _experts(route, xs, lw["w_up"], lw["b_up"], lw["w_down"], lw["b_down"], bm=bm)
    nt = t // tm
    slots = jnp.transpose(slot[:, :TOP_K].reshape(nt, tm, TOP_K), (0, 2, 1)).reshape(nt, 1, TOP_K * tm)
    return _tail(slots, x1, ys, gates, p_tok, lw["g_ple"], lw["w_ple_gate"], lw["w_ple_proj"], lw["g_final"], tm=tm)


def kernel(x_prompt, x_sample, p_prompt, p_sample, cache_k, cache_v, cache_logf, state_ssm_re, state_ssm_im,
           page_table, g_mix, w_in, b_fgate, lam_re, lam_im, log_step, b_ssm_re, b_ssm_im, c_ssm_re, c_ssm_im,
           d_ssm, w_glu, b_glu, g_attn_out, g_ssm_out, w_out, g_ffn, w_router, b_router, w_up, b_up, w_down,
           b_down, g_ple, w_ple_gate, w_ple_proj, g_final):
    depth = w_in.shape[0]
    assert depth == 1
    bp, sp, d = x_prompt.shape
    bd, sd, _ = x_sample.shape
    n_heads = b_fgate.shape[1]
    aw = n_heads * HEAD_DIM
    n_groups, n_state = lam_re.shape[1:]
    sw = n_groups * SSM_GROUP
    half = n_groups * n_state
    assert sw == SSM_LANE_BLOCKS * LANES and w_in.shape[2] == 3 * aw + n_heads + sw

    wi = w_in[0]
    wq, wk, wv = wi[:, :aw], wi[:, aw:2 * aw], wi[:, 2 * aw:3 * aw]
    wf, wu = wi[:, 3 * aw:3 * aw + n_heads], wi[:, 3 * aw + n_heads:]
    w_all = jnp.concatenate([wq, wk, wv, wu, jnp.pad(wf, ((0, 0), (0, LANES - n_heads)))], axis=1).astype(BF16)
    wft = jnp.pad(wf.T, ((0, 2 * SUBLANES - n_heads), (0, 0))).astype(BF16)
    bf_col = jnp.pad(b_fgate[0][:, None], ((0, 2 * SUBLANES - n_heads), (0, 0)))
    bf_row = jnp.pad(b_fgate, ((0, 0), (0, LANES - n_heads)))

    abar_re, abar_im, bbr_t, bbi_t = _s5_params(
        lam_re[0], lam_im[0], log_step[0], jnp.transpose(b_ssm_re[0], (0, 2, 1)), jnp.transpose(b_ssm_im[0], (0, 2, 1)))
    gpb = n_groups // SSM_LANE_BLOCKS
    blk4 = lambda a: a.reshape((SSM_LANE_BLOCKS, gpb) + a.shape[1:])
    bd_mat = jnp.concatenate([_block_diag(blk4(bbr_t), gpb), _block_diag(blk4(bbi_t), gpb)], axis=-1).astype(BF16)
    cdr = _block_diag(blk4(jnp.transpose(c_ssm_re[0], (0, 2, 1))), gpb).astype(BF16)
    cdi = _block_diag(blk4(jnp.transpose(c_ssm_im[0], (0, 2, 1))), gpb).astype(BF16)
    ssm_w = (abar_re.reshape(1, half), abar_im.reshape(1, half), bd_mat, cdr, cdi, d_ssm[0].reshape(1, sw),
             w_glu[0].astype(BF16), b_glu, g_ssm_out)

    lw = dict(g_attn=g_attn_out, wo_a=w_out[0, :aw].astype(BF16), wo_s=w_out[0, aw:].astype(BF16), g_ffn=g_ffn,
              wr_pad=jnp.pad(w_router[0], ((0, 0), (0, LANES - N_EXPERTS))).astype(BF16),
              br_pad=jnp.pad(b_router, ((0, 0), (0, LANES - N_EXPERTS))),
              w_up=w_up[0], b_up=b_up[0][:, None, :], w_down=w_down[0],
              b_down=b_down[0][:, None, :], g_ple=g_ple, w_ple_gate=w_ple_gate[0].astype(BF16),
              w_ple_proj=w_ple_proj[0].astype(BF16), g_final=g_final.reshape(1, d))

    q, k, v, kb, vb, u, lft, ct = _inproj_prompt(x_prompt, g_mix, w_all, wft, bf_col, aw=aw, n_heads=n_heads, tm=512)
    attn = _fox_prompt(q, kb, vb, ct, tq=512)
    n_time = min(64, sp)
    zeros_h = jnp.zeros((bp, half), F32)
    ssm_tb, hr_p, hi_p = _ssm(jnp.transpose(u, (1, 0, 2)).reshape(sp * bp, sw), zeros_h, zeros_h, *ssm_w,
                              n_batch=bp, n_time=n_time)
    ssm_p = jnp.transpose(ssm_tb.reshape(sp, bp, sw), (1, 0, 2)).reshape(bp * sp, sw)
    y_p = _mixer_tail(x_prompt.reshape(bp * sp, d), attn.reshape(bp * sp, aw), ssm_p,
                      p_prompt[0].reshape(bp * sp, -1), lw, tm=512, bm=512)

    ts = bd * sd
    qs, ks, vs, us, lfs = _inproj_sample(x_sample.reshape(ts, d), g_mix, w_all, bf_row, aw=aw)
    lfs = lfs[:, :n_heads]
    kt_cache = jnp.transpose(cache_k[0], (0, 2, 3, 1))
    vt_cache = jnp.transpose(cache_v[0], (0, 2, 3, 1))
    lft_cache = jnp.transpose(cache_logf[0], (0, 2, 1))
    attn_s = _fox_decode(page_table, qs.reshape(bd, sd, aw), ks.reshape(bd, sd, aw), vs.reshape(bd, sd, aw),
                         jnp.transpose(lfs.reshape(bd, sd, n_heads), (0, 2, 1)), kt_cache, vt_cache, lft_cache,
                         n_group=16)
    ssm_s_tb, hr_s, hi_s = _ssm(jnp.transpose(us.reshape(bd, sd, sw), (1, 0, 2)).reshape(ts, sw),
                                state_ssm_re[0].reshape(bd, half), state_ssm_im[0].reshape(bd, half), *ssm_w,
                                n_batch=bd, n_time=sd)
    ssm_s = jnp.transpose(ssm_s_tb.reshape(sd, bd, sw), (1, 0, 2)).reshape(ts, sw)
    y_s = _mixer_tail(x_sample.reshape(ts, d), attn_s.reshape(ts, aw), ssm_s, p_sample[0].reshape(ts, -1), lw,
                      tm=128, bm=128)

    heads = lambda a, b_, s_: a.reshape(1, b_, s_, n_heads, HEAD_DIM)
    state = lambda a, b_: a.reshape(1, b_, n_groups, n_state)
    return (y_p.reshape(bp, sp, d), y_s.reshape(bd, sd, d),
            heads(k, bp, sp), heads(v, bp, sp), jnp.transpose(lft, (0, 2, 1))[None],
            state(hr_p, bp), state(hi_p, bp),
            heads(ks, bd, sd), heads(vs, bd, sd), lfs.reshape(1, bd, sd, n_heads),
            state(hr_s, bd), state(hi_s, bd))
```

```python
import functools

import jax
import jax.numpy as jnp
from jax import lax
from jax.experimental import pallas as pl
from jax.experimental.pallas import tpu as pltpu

F32 = jnp.float32
BF16 = jnp.bfloat16
I32 = jnp.int32

RMS_EPS = 1e-6
LOG2E = 1.4426950408889634
HEAD_DIM = 64
HEADS_PER_LANE_TILE = 2
LANES = 128
SUBLANES = 8
SSM_GROUP = 16
SSM_STATE = 64
SSM_LANE_BLOCKS = 4
N_EXPERTS = 32
TOP_K = 4
GRANULE = 8
SWIGLU_ALPHA = 1.702
SWIGLU_LIMIT = 7.0
VMEM_LIMIT = 56 * 1024 * 1024


def _params(sem, vmem=VMEM_LIMIT):
    return pltpu.CompilerParams(dimension_semantics=sem, vmem_limit_bytes=vmem)


def _rms(x, g):
    return x * lax.rsqrt(jnp.mean(x * x, axis=-1, keepdims=True) + RMS_EPS) * g


def _log_sigmoid(x):
    return jnp.minimum(x, 0.0) - jnp.log1p(jnp.exp(-jnp.abs(x)))


def _split3(x):
    hi = x.astype(BF16)
    r1 = x - hi.astype(F32)
    mid = r1.astype(BF16)
    lo = (r1 - mid.astype(F32)).astype(BF16)
    return hi, mid, lo


def _dot(a, b):
    return jnp.dot(a, b, preferred_element_type=F32)


def _dot_nt(a, b):
    return lax.dot_general(a, b, (((1,), (1,)), ((), ())), preferred_element_type=F32)


def _dot3(x, m):
    hi, mid, lo = _split3(x)
    return _dot(hi, m) + _dot(mid, m) + _dot(lo, m)


def _s5_params_kernel(lr_ref, li_ref, ls_ref, brt_ref, bit_ref, ar_ref, ai_ref, bbr_ref, bbi_ref):
    lr, li = lr_ref[...], li_ref[...]
    step = jnp.exp(ls_ref[...])
    mag = jnp.exp(lr * step)
    a_re, a_im = mag * jnp.cos(li * step), mag * jnp.sin(li * step)
    nr, ni = a_re - 1.0, a_im
    den = lr * lr + li * li
    coef_re = (nr * lr + ni * li) / den
    coef_im = (ni * lr - nr * li) / den
    ar_ref[...] = a_re
    ai_ref[...] = a_im
    br, bi = brt_ref[...], bit_ref[...]
    bbr_ref[...] = coef_re * br - coef_im * bi
    bbi_ref[...] = coef_re * bi + coef_im * br


def _s5_params(lam_re, lam_im, log_step, b_re_t, b_im_t):
    g, p = lam_re.shape
    c = b_re_t.shape[1]
    return pl.pallas_call(
        _s5_params_kernel,
        out_shape=(jax.ShapeDtypeStruct((g, 1, p), F32), jax.ShapeDtypeStruct((g, 1, p), F32),
                   jax.ShapeDtypeStruct((g, c, p), F32), jax.ShapeDtypeStruct((g, c, p), F32)),
        name="s5_params",
    )(lam_re.reshape(g, 1, p), lam_im.reshape(g, 1, p), log_step.reshape(g, 1, 1), b_re_t, b_im_t)


def _inproj_core(x, g_ref, w_ref, aw):
    h = _rms(x, g_ref[...]).astype(BF16)
    z = _dot(h, w_ref[...])
    return h, z


def _inproj_prompt_kernel(x_ref, g_ref, w_ref, wft_ref, bf_ref, utri_ref,
                          q_ref, k_ref, v_ref, kb_ref, vb_ref, u_ref, lft_ref, ct_ref, carry_ref, *, aw, scale):
    tm = x_ref.shape[1]
    h, z = _inproj_core(x_ref[0], g_ref, w_ref, aw)
    q_ref[0] = (z[:, :aw] * scale).astype(BF16)
    k = z[:, aw:2 * aw]
    v = z[:, 2 * aw:3 * aw]
    k_ref[0] = k
    v_ref[0] = v
    kb_ref[0] = k.astype(BF16)
    vb_ref[0] = v.astype(BF16)
    u_ref[0] = z[:, 3 * aw:4 * aw]
    ft = _dot_nt(wft_ref[...], h) + bf_ref[...]
    lft = _log_sigmoid(ft)
    n_heads = lft_ref.shape[1]
    lft_ref[0] = lft[:n_heads]

    @pl.when(pl.program_id(1) == 0)
    def _():
        carry_ref[...] = jnp.zeros_like(carry_ref)

    cs = _dot3(lft, utri_ref[...]) + carry_ref[:, :1]
    carry_ref[...] = jnp.broadcast_to(cs[:, tm - 1:tm], carry_ref.shape)
    ct_ref[0] = cs[:n_heads]


def _inproj_prompt(x, g_mix, w_all, wft, bf_col, *, aw, n_heads, tm):
    b, s, d = x.shape
    tm = min(tm, s)
    assert s % tm == 0
    ii = lax.broadcasted_iota(I32, (tm, tm), 0)
    jj = lax.broadcasted_iota(I32, (tm, tm), 1)
    utri = (ii <= jj).astype(BF16)
    row = lambda bi, si: (bi, si, 0)
    col = lambda bi, si: (bi, 0, si)
    const = lambda bi, si: (0, 0)
    wide = pl.BlockSpec((1, tm, aw), row)
    tall = pl.BlockSpec((1, n_heads, tm), col)
    return pl.pallas_call(
        functools.partial(_inproj_prompt_kernel, aw=aw, scale=HEAD_DIM ** -0.5 * LOG2E),
        grid=(b, s // tm),
        in_specs=[pl.BlockSpec((1, tm, d), row), pl.BlockSpec((1, d), const), pl.BlockSpec(w_all.shape, const),
                  pl.BlockSpec(wft.shape, const), pl.BlockSpec(bf_col.shape, const), pl.BlockSpec((tm, tm), const)],
        out_specs=[wide, wide, wide, wide, wide, wide, tall, tall],
        out_shape=[jax.ShapeDtypeStruct((b, s, aw), BF16), jax.ShapeDtypeStruct((b, s, aw), F32),
                   jax.ShapeDtypeStruct((b, s, aw), F32), jax.ShapeDtypeStruct((b, s, aw), BF16),
                   jax.ShapeDtypeStruct((b, s, aw), BF16), jax.ShapeDtypeStruct((b, s, aw), F32),
                   jax.ShapeDtypeStruct((b, n_heads, s), F32), jax.ShapeDtypeStruct((b, n_heads, s), F32)],
        scratch_shapes=[pltpu.VMEM((2 * SUBLANES, LANES), F32)],
        compiler_params=_params(("arbitrary", "arbitrary")),
        name="inproj_prompt",
    )(x, g_mix, w_all, wft, bf_col, utri)


def _inproj_sample_kernel(x_ref, g_ref, w_ref, bf_ref, q_ref, k_ref, v_ref, u_ref, lf_ref, *, aw, scale):
    _, z = _inproj_core(x_ref[...], g_ref, w_ref, aw)
    q_ref[...] = (z[:, :aw] * scale).astype(BF16)
    k_ref[...] = z[:, aw:2 * aw]
    v_ref[...] = z[:, 2 * aw:3 * aw]
    u_ref[...] = z[:, 3 * aw:4 * aw]
    lf_ref[...] = _log_sigmoid(z[:, 4 * aw:] + bf_ref[...])


def _inproj_sample(x, g_mix, w_all, bf_row, *, aw):
    t, d = x.shape
    outs = [jax.ShapeDtypeStruct((t, aw), BF16)] + [jax.ShapeDtypeStruct((t, aw), F32)] * 3
    outs.append(jax.ShapeDtypeStruct((t, LANES), F32))
    return pl.pallas_call(
        functools.partial(_inproj_sample_kernel, aw=aw, scale=HEAD_DIM ** -0.5),
        out_shape=outs, compiler_params=_params(None), name="inproj_sample",
    )(x, g_mix, w_all, bf_row)


_L_ONE, _L_BIAS = HEAD_DIM, HEAD_DIM + 3


def _bias_rows(c_row, piece_row0, ones_row0, n_ones, width):
    hi, mid, lo = (p.astype(F32) for p in _split3(c_row))
    r = lax.broadcasted_iota(I32, (LANES, width), 0)
    out = jnp.where((r >= ones_row0) & (r < ones_row0 + n_ones), 1.0, 0.0)
    for i, piece in enumerate((hi, mid, lo)):
        out = jnp.where(r == piece_row0 + i, piece, out)
    return out


def _fox_prompt_kernel(q_ref, k_ref, v_ref, ct_ref, o_ref, kaug, vaug, *, tq):
    qi = pl.program_id(2)
    pair = pl.program_id(1)
    n_kv = k_ref.shape[1] // tq
    lane = lax.broadcasted_iota(I32, (tq, LANES), 1)
    row = lax.broadcasted_iota(I32, (tq, tq), 0)
    colm = lax.broadcasted_iota(I32, (tq, tq), 1)
    src = lax.broadcasted_iota(I32, (LANES, LANES), 0)
    dst = lax.broadcasted_iota(I32, (LANES, LANES), 1)
    sels = [((src == dst + hh * HEAD_DIM) & (dst < HEAD_DIM)).astype(BF16) for hh in range(HEADS_PER_LANE_TILE)]

    @pl.when(qi == 0)
    def _():
        for hh in range(HEADS_PER_LANE_TILE):
            head = pair * HEADS_PER_LANE_TILE + hh
            for c in range(n_kv):
                rows = slice(c * tq, (c + 1) * tq)
                ck = ct_ref[0, head, c:c + 1, :] * LOG2E
                kb = _bias_rows(-ck, _L_ONE, _L_BIAS, 3, tq).T
                kaug[hh, rows, :] = (_dot(k_ref[0, rows, :], sels[hh]) + kb).astype(BF16)
                vaug[hh, rows, :] = (_dot(v_ref[0, rows, :], sels[hh])
                                     + jnp.where(lane == _L_ONE, 1.0, 0.0)).astype(BF16)

    qas = []
    for hh in range(HEADS_PER_LANE_TILE):
        cq = ct_ref[0, pair * HEADS_PER_LANE_TILE + hh, pl.ds(qi, 1), :] * LOG2E
        qas.append((_dot(q_ref[0], sels[hh]) + _bias_rows(cq, _L_BIAS, _L_ONE, 3, tq).T).astype(BF16))

    def tile(j, carry, masked):
        start = pl.multiple_of(j * tq, tq)
        new = []
        for hh in range(HEADS_PER_LANE_TILE):
            m, acc = carry[hh]
            ka = kaug[hh, pl.ds(start, tq), :]
            va = vaug[hh, pl.ds(start, tq), :]
            s = _dot_nt(qas[hh], ka)
            if masked:
                s = jnp.where(colm <= row, s, -jnp.inf)
            m_new = jnp.maximum(m, jnp.max(s, axis=-1, keepdims=True))
            p = jnp.exp2(s - m_new).astype(BF16)
            acc = jnp.exp2(m - m_new) * acc + _dot(p, va)
            new.append((m_new, acc))
        return tuple(new)

    init = tuple((jnp.full((tq, 1), -jnp.inf, F32), jnp.zeros((tq, LANES), F32))
                 for _ in range(HEADS_PER_LANE_TILE))
    carry = lax.fori_loop(0, qi // 2, lambda jj, c: tile(2 * jj + 1, tile(2 * jj, c, False), False), init)
    carry = lax.cond(qi % 2 == 1, lambda c: tile(qi - 1, c, False), lambda c: c, carry)
    (_, acc0), (_, acc1) = tile(qi, carry, True)
    out0 = acc0 / acc0[:, _L_ONE:_L_ONE + 1]
    out1 = acc1 / acc1[:, _L_ONE:_L_ONE + 1]
    o_ref[0] = jnp.where(lane < HEAD_DIM, out0, pltpu.roll(out1, HEAD_DIM, 1))


def _fox_prompt(q, kb, vb, ct, *, tq):
    b, s, aw = q.shape
    tq = min(tq, s)
    assert s % tq == 0
    n_pairs = aw // LANES
    ct = ct.reshape(b, ct.shape[1], s // tq, tq)
    return pl.pallas_call(
        functools.partial(_fox_prompt_kernel, tq=tq),
        grid=(b, n_pairs, s // tq),
        in_specs=[pl.BlockSpec((1, tq, LANES), lambda bi, p, qi: (bi, qi, p)),
                  pl.BlockSpec((1, s, LANES), lambda bi, p, qi: (bi, 0, p)),
                  pl.BlockSpec((1, s, LANES), lambda bi, p, qi: (bi, 0, p)),
                  pl.BlockSpec((1,) + ct.shape[1:], lambda bi, p, qi: (bi, 0, 0, 0))],
        out_specs=pl.BlockSpec((1, tq, LANES), lambda bi, p, qi: (bi, qi, p)),
        out_shape=jax.ShapeDtypeStruct((b, s, aw), F32),
        scratch_shapes=[pltpu.VMEM((HEADS_PER_LANE_TILE, s, LANES), BF16),
                        pltpu.VMEM((HEADS_PER_LANE_TILE, s, LANES), BF16)],
        compiler_params=_params(("arbitrary", "arbitrary", "arbitrary")),
        name="fox_prompt",
    )(q, kb, vb, ct)


def _fox_decode_kernel(pt_ref, q_ref, kn_ref, vn_ref, lfn_ref, ustrict_ref, *rest, n_group, n_new):
    kt_refs = rest[:n_group]
    vt_refs = rest[n_group:2 * n_group]
    lf_refs = rest[2 * n_group:3 * n_group]
    o_ref, m_ref, l_ref, acc_ref, tail_ref = rest[3 * n_group:]
    j = pl.program_id(1)
    n_heads = lfn_ref.shape[1]
    width = q_ref.shape[2]
    rows = n_new * n_heads

    lane_head = lax.broadcasted_iota(I32, (n_heads, width), 1) // HEAD_DIM
    sub = lax.broadcasted_iota(I32, (n_heads, width), 0)
    hmask = lane_head == sub
    q = q_ref[0].astype(F32)
    qexp = jnp.concatenate([jnp.where(hmask, jnp.broadcast_to(q[t:t + 1], (n_heads, width)), 0.0)
                            for t in range(n_new)], axis=0)
    trow = lax.broadcasted_iota(I32, (rows, 1), 0) // n_heads

    @pl.when(j == 0)
    def _():
        lfn = lfn_ref[0]
        kn, vn = kn_ref[0], vn_ref[0]
        run = jnp.zeros((n_heads, 1), F32)
        logits = []
        for sp in range(n_new):
            run = run + lfn[:, sp:sp + 1]
            bias = jnp.concatenate([run] * n_new, axis=0)
            sc = jnp.sum(qexp * kn[sp:sp + 1], axis=-1, keepdims=True) - bias
            logits.append(jnp.where(trow >= sp, sc, -jnp.inf))
        m = logits[0]
        for sc in logits[1:]:
            m = jnp.maximum(m, sc)
        l = jnp.zeros((rows, 1), F32)
        acc = jnp.zeros((rows, width), F32)
        for sp in range(n_new):
            p = jnp.exp(logits[sp] - m)
            l = l + p
            acc = acc + p * vn[sp:sp + 1]
        m_ref[...] = m
        l_ref[...] = l
        acc_ref[...] = acc
        tail_ref[...] = jnp.zeros_like(tail_ref)

    qb = qexp.astype(BF16)
    tail = tail_ref[:, :1]
    scores = []
    for g in range(n_group):
        kt = kt_refs[g][0].reshape(width, -1).astype(BF16)
        lf = lf_refs[g][0]
        lf_pad = jnp.concatenate([lf, jnp.zeros_like(lf)], axis=0)
        suffix = _dot3(lf_pad, ustrict_ref[...])[:n_heads] + tail
        tail = tail + jnp.sum(lf, axis=-1, keepdims=True)
        scores.append(_dot(qb, kt) + jnp.concatenate([suffix] * n_new, axis=0))
    tail_ref[...] = jnp.broadcast_to(tail, tail_ref.shape)
    m_old = m_ref[...]
    m_new = m_old
    for s in scores:
        m_new = jnp.maximum(m_new, jnp.max(s, axis=-1, keepdims=True))
    alpha = jnp.exp(m_old - m_new)
    l = alpha * l_ref[...]
    acc = alpha * acc_ref[...]
    for g, s in enumerate(scores):
        p = jnp.exp(s - m_new)
        l = l + jnp.sum(p, axis=-1, keepdims=True)
        acc = acc + _dot_nt(p.astype(BF16), vt_refs[g][0].reshape(width, -1).astype(BF16))
    l_ref[...] = l
    acc_ref[...] = acc
    m_ref[...] = m_new

    @pl.when(j == pl.num_programs(1) - 1)
    def _():
        o = acc_ref[...] / l_ref[...]
        outs = [jnp.sum(jnp.where(hmask, o[t * n_heads:(t + 1) * n_heads], 0.0), axis=0, keepdims=True)
                for t in range(n_new)]
        o_ref[0] = jnp.concatenate(outs, axis=0)


def _fox_decode(page_table, q, k_new, v_new, lfn_t, kt_cache, vt_cache, lft_cache, *, n_group):
    bd, n_new, width = q.shape
    n_pages = page_table.shape[1]
    n_heads, page = lft_cache.shape[1:]
    n_group = min(n_group, n_pages)
    assert n_pages % n_group == 0
    n_steps = n_pages // n_group
    ii = lax.broadcasted_iota(I32, (page, page), 0)
    jj = lax.broadcasted_iota(I32, (page, page), 1)
    ustrict = (ii > jj).astype(BF16)

    def page_map(g, ndim):
        def index_map(b, j, pt):
            return (pt[b, n_pages - 1 - (j * n_group + g)],) + (0,) * ndim
        return index_map

    per_batch = lambda b, j, pt: (b, 0, 0)
    in_specs = [pl.BlockSpec((1, n_new, width), per_batch), pl.BlockSpec((1, n_new, width), per_batch),
                pl.BlockSpec((1, n_new, width), per_batch), pl.BlockSpec((1, n_heads, n_new), per_batch),
                pl.BlockSpec((page, page), lambda b, j, pt: (0, 0))]
    in_specs += [pl.BlockSpec((1,) + kt_cache.shape[1:], page_map(g, 3)) for g in range(n_group)]
    in_specs += [pl.BlockSpec((1,) + vt_cache.shape[1:], page_map(g, 3)) for g in range(n_group)]
    in_specs += [pl.BlockSpec((1, n_heads, page), page_map(g, 2)) for g in range(n_group)]
    rows = n_new * n_heads
    grid_spec = pltpu.PrefetchScalarGridSpec(
        num_scalar_prefetch=1, grid=(bd, n_steps), in_specs=in_specs,
        out_specs=pl.BlockSpec((1, n_new, width), per_batch),
        scratch_shapes=[pltpu.VMEM((rows, 1), F32), pltpu.VMEM((rows, 1), F32), pltpu.VMEM((rows, width), F32),
                        pltpu.VMEM((n_heads, LANES), F32)])
    return pl.pallas_call(
        functools.partial(_fox_decode_kernel, n_group=n_group, n_new=n_new),
        grid_spec=grid_spec, out_shape=jax.ShapeDtypeStruct((bd, n_new, width), F32),
        compiler_params=_params(("arbitrary", "arbitrary")), name="fox_decode",
    )(page_table, q, k_new, v_new, lfn_t, ustrict,
      *([kt_cache] * n_group), *([vt_cache] * n_group), *([lft_cache] * n_group))


def _ssm_kernel(u_ref, h0r_ref, h0i_ref, ar_ref, ai_ref, bd_ref, cdr_ref, cdi_ref, d_ref, wglu_ref, bglu_ref,
                gout_ref, o_ref, hr_ref, hi_ref, bu_ref, h_ref, *, n_batch, n_time):
    i = pl.program_id(0)
    half = ar_ref.shape[1]
    blk = half // SSM_LANE_BLOCKS
    rows = n_batch * n_time

    @pl.when(i == 0)
    def _():
        h_ref[:, :half] = h0r_ref[...]
        h_ref[:, half:] = h0i_ref[...]

    u = u_ref[...]
    ub = u.astype(BF16)
    ys = []
    for jb in range(SSM_LANE_BLOCKS):
        lo_r, lo_i = jb * blk, half + jb * blk
        bu = _dot(ub[:, jb * LANES:(jb + 1) * LANES], bd_ref[jb])
        bu_ref[:, lo_r:lo_r + blk] = bu[:, :blk]
        bu_ref[:, lo_i:lo_i + blk] = bu[:, blk:]
        a_r = jnp.broadcast_to(ar_ref[:, lo_r:lo_r + blk], (SUBLANES, blk))
        a_i = jnp.broadcast_to(ai_ref[:, lo_r:lo_r + blk], (SUBLANES, blk))
        for bg in range(n_batch // SUBLANES):
            b0 = bg * SUBLANES
            h_r = h_ref[b0:b0 + SUBLANES, lo_r:lo_r + blk]
            h_i = h_ref[b0:b0 + SUBLANES, lo_i:lo_i + blk]
            for t in range(n_time):
                r0 = t * n_batch + b0
                n_r = a_r * h_r - a_i * h_i + bu_ref[r0:r0 + SUBLANES, lo_r:lo_r + blk]
                n_i = a_r * h_i + a_i * h_r + bu_ref[r0:r0 + SUBLANES, lo_i:lo_i + blk]
                bu_ref[r0:r0 + SUBLANES, lo_r:lo_r + blk] = n_r
                bu_ref[r0:r0 + SUBLANES, lo_i:lo_i + blk] = n_i
                h_r, h_i = n_r, n_i
            h_ref[b0:b0 + SUBLANES, lo_r:lo_r + blk] = h_r
            h_ref[b0:b0 + SUBLANES, lo_i:lo_i + blk] = h_i
        hre = bu_ref[:, lo_r:lo_r + blk].astype(BF16)
        him = bu_ref[:, lo_i:lo_i + blk].astype(BF16)
        ys.append(_dot(hre, cdr_ref[jb]) - _dot(him, cdi_ref[jb]))
    y = jnp.concatenate(ys, axis=-1) + d_ref[...] * u
    y = jax.nn.gelu(y, approximate=True)
    y = y * jax.nn.sigmoid(_dot(y.astype(BF16), wglu_ref[...]) + bglu_ref[...])
    o_ref[...] = _rms(y, gout_ref[...]).astype(BF16)

    @pl.when(i == pl.num_programs(0) - 1)
    def _():
        hr_ref[...] = h_ref[:, :half]
        hi_ref[...] = h_ref[:, half:]


def _ssm(u_tb, h0r, h0i, abar_re, abar_im, bd, cdr, cdi, d_row, w_glu, b_glu, g_out, *, n_batch, n_time):
    total, width = u_tb.shape
    half = abar_re.shape[1]
    rows = n_batch * n_time
    assert total % rows == 0 and n_batch % SUBLANES == 0
    const2 = lambda i: (0, 0)
    const3 = lambda i: (0, 0, 0)
    full = lambda a: pl.BlockSpec(a.shape, const2 if a.ndim == 2 else const3)
    return pl.pallas_call(
        functools.partial(_ssm_kernel, n_batch=n_batch, n_time=n_time),
        grid=(total // rows,),
        in_specs=[pl.BlockSpec((rows, width), lambda i: (i, 0))] + [full(a) for a in (
            h0r, h0i, abar_re, abar_im, bd, cdr, cdi, d_row, w_glu, b_glu, g_out)],
        out_specs=[pl.BlockSpec((rows, width), lambda i: (i, 0)), pl.BlockSpec((n_batch, half), const2),
                   pl.BlockSpec((n_batch, half), const2)],
        out_shape=[jax.ShapeDtypeStruct((total, width), BF16), jax.ShapeDtypeStruct((n_batch, half), F32),
                   jax.ShapeDtypeStruct((n_batch, half), F32)],
        scratch_shapes=[pltpu.VMEM((rows, 2 * half), F32), pltpu.VMEM((n_batch, 2 * half), F32)],
        compiler_params=_params(("arbitrary",)), name="ssm",
    )(u_tb, h0r, h0i, abar_re, abar_im, bd, cdr, cdi, d_row, w_glu, b_glu, g_out)


def _outproj_kernel(x_ref, a_ref, s_ref, ga_ref, woa_ref, wos_ref, gffn_ref, wr_ref, br_ref, lstrict_ref, cnt0_ref,
                    x1_ref, hm_ref, idx_ref, gate_ref, rank_ref, before_ref, cnt_ref, cnt_sc):
    i = pl.program_id(0)
    tm = x_ref.shape[0]

    @pl.when(i == 0)
    def _():
        cnt_sc[...] = cnt0_ref[...]

    before_ref[0] = cnt_sc[...]

    an = _rms(a_ref[...], ga_ref[...]).astype(BF16)
    x1 = x_ref[...] + _dot(an, woa_ref[...]) + _dot(s_ref[...], wos_ref[...])
    x1_ref[...] = x1
    hm = _rms(x1, gffn_ref[...])
    hm_ref[...] = hm
    logits = _dot(hm.astype(BF16), wr_ref[...]) + br_ref[...]
    lane_i = lax.broadcasted_iota(I32, (tm, LANES), 1)
    lane = lane_i.astype(F32)
    work = jnp.where(lane_i < N_EXPERTS, logits, -jnp.inf)
    vals, idxs = [], []
    for _ in range(TOP_K):
        mx = jnp.max(work, axis=-1, keepdims=True)
        ix = jnp.min(jnp.where(work == mx, lane, float(LANES)), axis=-1, keepdims=True)
        vals.append(mx)
        idxs.append(ix)
        work = jnp.where(lane == ix, -jnp.inf, work)
    exps = [jnp.exp(v - vals[0]) for v in vals]
    denom = exps[0]
    for e in exps[1:]:
        denom = denom + e
    onehot = jnp.zeros((tm, LANES), F32)
    for ix in idxs:
        onehot = onehot + (lane == ix).astype(F32)
    before = _dot(lstrict_ref[...], onehot.astype(BF16)) + cnt_sc[:1, :]
    idx_out = jnp.zeros((tm, LANES), I32)
    gate_out = jnp.zeros((tm, LANES), F32)
    rank_out = jnp.zeros((tm, LANES), I32)
    for k in range(TOP_K):
        rk = jnp.sum(jnp.where(lane == idxs[k], before, 0.0), axis=-1, keepdims=True)
        idx_out = jnp.where(lane_i == k, idxs[k].astype(I32), idx_out)
        gate_out = jnp.where(lane_i == k, exps[k] / denom, gate_out)
        rank_out = jnp.where(lane_i == k, rk.astype(I32), rank_out)
    idx_ref[...] = idx_out
    gate_ref[...] = gate_out
    rank_ref[...] = rank_out
    cnt = cnt_sc[:1, :] + jnp.sum(onehot, axis=0, keepdims=True)
    cnt_sc[...] = jnp.broadcast_to(cnt, cnt_sc.shape)
    cnt_ref[...] = jnp.broadcast_to(cnt, cnt_ref.shape)


def _outproj(x, attn, ssm_n, g_attn, wo_a, wo_s, g_ffn, wr_pad, br_pad, cnt0, *, tm):
    t, d = x.shape
    assert t % tm == 0
    ii = lax.broadcasted_iota(I32, (tm, tm), 0)
    jj = lax.broadcasted_iota(I32, (tm, tm), 1)
    lstrict = (jj < ii).astype(BF16)
    rowsp = lambda w: pl.BlockSpec((tm, w), lambda i: (i, 0))
    full = lambda a: pl.BlockSpec(a.shape, lambda i: (0, 0))
    aw = attn.shape[1]
    return pl.pallas_call(
        _outproj_kernel, grid=(t // tm,),
        in_specs=[rowsp(d), rowsp(aw), rowsp(ssm_n.shape[1])] + [full(a) for a in (
            g_attn, wo_a, wo_s, g_ffn, wr_pad, br_pad, lstrict, cnt0)],
        out_specs=[rowsp(d), rowsp(d), rowsp(LANES), rowsp(LANES), rowsp(LANES),
                   pl.BlockSpec((1, SUBLANES, LANES), lambda i: (i, 0, 0)),
                   pl.BlockSpec((SUBLANES, LANES), lambda i: (0, 0))],
        out_shape=[jax.ShapeDtypeStruct((t, d), F32), jax.ShapeDtypeStruct((t, d), F32),
                   jax.ShapeDtypeStruct((t, LANES), I32), jax.ShapeDtypeStruct((t, LANES), F32),
                   jax.ShapeDtypeStruct((t, LANES), I32), jax.ShapeDtypeStruct((t // tm, SUBLANES, LANES), F32),
                   jax.ShapeDtypeStruct((SUBLANES, LANES), F32)],
        scratch_shapes=[pltpu.VMEM((SUBLANES, LANES), F32)],
        compiler_params=_params(("arbitrary",)), name="outproj_router",
    )(x, attn, ssm_n, g_attn, wo_a, wo_s, g_ffn, wr_pad, br_pad, lstrict, cnt0)


def _dispatch_kernel(gseg_ref, lstart_ref, ngran_ref, total_ref, hm_ref, idx_ref, rank_ref, before_ref, lstartv_ref,
                     gsegv_ref, *rest, chunk, has_prev, fill_tail):
    xs_ref, slot_ref, sorted_ref, zeros_ref, sem = (rest[:1] + rest[2:]) if has_prev else rest
    i = pl.program_id(0)
    tm = hm_ref.shape[0]
    lane = lax.broadcasted_iota(I32, (tm, LANES), 1)
    idx = idx_ref[...]
    rank = rank_ref[...].astype(F32)
    local_off = lstartv_ref[0][:1] - before_ref[0][:1]
    global_off = gsegv_ref[0][:1] - before_ref[0][:1]
    lslots = jnp.full((tm, LANES), -1.0, F32)
    gslots = jnp.zeros((tm, LANES), I32)
    for k in range(TOP_K):
        sel = lane == idx[:, k:k + 1]
        r_k = rank[:, k:k + 1]
        ls = jnp.sum(jnp.where(sel, local_off, 0.0), axis=-1, keepdims=True) + r_k
        gs = jnp.sum(jnp.where(sel, global_off, 0.0), axis=-1, keepdims=True) + r_k
        lslots = jnp.where(lane == k, ls, lslots)
        gslots = jnp.where(lane == k, gs.astype(I32), gslots)
    slot_ref[...] = gslots
    lt = lslots.T
    hb = hm_ref[...].astype(BF16)
    for c in range(sorted_ref.shape[0] // chunk):
        pos = (c * chunk + lax.broadcasted_iota(I32, (chunk, tm), 0)).astype(F32)
        perm = jnp.zeros((chunk, tm), F32)
        for k in range(TOP_K):
            perm = perm + jnp.where(pos == lt[k:k + 1], 1.0, 0.0)
        sorted_ref[c * chunk:(c + 1) * chunk, :] = _dot(perm.astype(BF16), hb)

    def copy(src, dst):
        return pltpu.make_async_copy(sorted_ref.at[pl.ds(src, GRANULE)], xs_ref.at[pl.ds(dst, GRANULE)], sem.at[0])

    total = 0
    for e in range(N_EXPERTS):
        n = ngran_ref[i * N_EXPERTS + e]
        src0 = lstart_ref[i * N_EXPERTS + e]
        dst0 = gseg_ref[i * N_EXPERTS + e]

        def body(g, c, src0=src0, dst0=dst0):
            copy(pl.multiple_of(src0 + g * GRANULE, GRANULE), pl.multiple_of(dst0 + g * GRANULE, GRANULE)).start()
            return c
        lax.fori_loop(0, n, body, 0)
        total = total + n

    def wait_one(g, c):
        copy(0, 0).wait()
        return c
    lax.fori_loop(0, total, wait_one, 0)

    if fill_tail:
        @pl.when(i == pl.num_programs(0) - 1)
        def _():
            zeros_ref[...] = jnp.zeros_like(zeros_ref)
            end = total_ref[0]

            def fill(g):
                dst = pl.multiple_of(end + g * GRANULE, GRANULE)
                return pltpu.make_async_copy(zeros_ref, xs_ref.at[pl.ds(dst, GRANULE)], sem.at[0])

            n_fill = (xs_ref.shape[0] - end) // GRANULE
            lax.fori_loop(0, n_fill, lambda g, c: (fill(g).start(), c)[1], 0)
            lax.fori_loop(0, n_fill, lambda g, c: (fill(g).wait(), c)[1], 0)


def _dispatch(route, tile0, hm, idx, rank, before, xs_prev, *, tm, fill_tail):
    t, d = hm.shape
    nt = t // tm
    n_local = TOP_K * tm + N_EXPERTS * GRANULE
    chunk = 256 if n_local % 256 == 0 else LANES
    assert n_local % chunk == 0
    has_prev = xs_prev is not None
    tiles = slice(tile0, tile0 + nt)
    flat = slice(tile0 * N_EXPERTS, (tile0 + nt) * N_EXPERTS)
    rowsp = lambda w: pl.BlockSpec((tm, w), lambda i, *_: (i, 0))
    tile3 = pl.BlockSpec((1, SUBLANES, LANES), lambda i, *_: (i, 0, 0))
    n_prefetch = 4
    in_specs = [rowsp(d), rowsp(LANES), rowsp(LANES), tile3, tile3, tile3]
    operands = [route["gseg"][flat], route["lstart"][flat], route["ngran"][flat], route["total"], hm, idx, rank,
                before, route["lstart_v"][tiles], route["gseg_v"][tiles]]
    aliases = {}
    if has_prev:
        in_specs.append(pl.BlockSpec(memory_space=pl.ANY))
        aliases = {len(operands): 0}
        operands.append(xs_prev)
    grid_spec = pltpu.PrefetchScalarGridSpec(
        num_scalar_prefetch=n_prefetch, grid=(nt,), in_specs=in_specs,
        out_specs=[pl.BlockSpec(memory_space=pl.ANY), rowsp(LANES)],
        scratch_shapes=[pltpu.VMEM((n_local, d), F32), pltpu.VMEM((GRANULE, d), F32),
                        pltpu.SemaphoreType.DMA((1,))])
    return pl.pallas_call(
        functools.partial(_dispatch_kernel, chunk=chunk, has_prev=has_prev, fill_tail=fill_tail),
        grid_spec=grid_spec, input_output_aliases=aliases,
        out_shape=[jax.ShapeDtypeStruct((route["n_xs"], d), F32), jax.ShapeDtypeStruct((t, LANES), I32)],
        compiler_params=_params(("arbitrary",)), name="dispatch",
    )(*operands)


def _experts_kernel(ie_ref, ib_ref, lo_ref, hi_ref, fl_ref, ni_ref, xs_ref, wup_ref, bup_ref, wdn_ref, bdn_ref,
                    ys_ref, wup_b, wdn_b, *, d_ff):
    w = pl.program_id(0)
    bm = xs_ref.shape[0]

    @pl.when(w >= ni_ref[0])
    def _():
        ys_ref[...] = jnp.zeros_like(ys_ref)

    @pl.when(w < ni_ref[0])
    def _():
        first = (fl_ref[w] & 1) == 1

        @pl.when((fl_ref[w] & 2) == 2)
        def _():
            step = 128
            for r in range(0, wup_b.shape[0], step):
                wup_b[r:r + step, :] = wup_ref[0, r:r + step, :].astype(BF16)
            for r in range(0, wdn_b.shape[0], step):
                wdn_b[r:r + step, :] = wdn_ref[0, r:r + step, :].astype(BF16)

        x = xs_ref[...].astype(BF16)
        up = _dot(x, wup_b[...]) + bup_ref[0]
        gate = jnp.minimum(up[:, :d_ff], SWIGLU_LIMIT)
        lin = jnp.clip(up[:, d_ff:], -SWIGLU_LIMIT, SWIGLU_LIMIT)
        act = gate * jax.nn.sigmoid(SWIGLU_ALPHA * gate) * (lin + 1.0)
        out = _dot(act.astype(BF16), wdn_b[...]) + bdn_ref[0]

        @pl.when(first)
        def _():
            ys_ref[...] = out

        @pl.when(jnp.logical_not(first))
        def _():
            rowi = lax.broadcasted_iota(I32, (bm, 1), 0)
            mine = (rowi >= lo_ref[w]) & (rowi < hi_ref[w])
            ys_ref[...] = jnp.where(mine, out, ys_ref[...])


def _experts(route, xs, w_up, b_up, w_down, b_down, *, bm):
    n_xs, d = xs.shape
    n_work = route["item_expert"].shape[0]
    d_ff = w_down.shape[1]
    by_block = lambda w, ie, ib, lo, hi, fl, ni: (ib[w], 0)
    by_expert = lambda w, ie, ib, lo, hi, fl, ni: (ie[w], 0, 0)
    grid_spec = pltpu.PrefetchScalarGridSpec(
        num_scalar_prefetch=6, grid=(n_work,),
        in_specs=[pl.BlockSpec((bm, d), by_block),
                  pl.BlockSpec((1,) + w_up.shape[1:], by_expert), pl.BlockSpec((1, 1, b_up.shape[2]), by_expert),
                  pl.BlockSpec((1,) + w_down.shape[1:], by_expert), pl.BlockSpec((1, 1, d), by_expert)],
        out_specs=pl.BlockSpec((bm, d), by_block),
        scratch_shapes=[pltpu.VMEM(w_up.shape[1:], BF16), pltpu.VMEM(w_down.shape[1:], BF16)])
    return pl.pallas_call(
        functools.partial(_experts_kernel, d_ff=d_ff),
        grid_spec=grid_spec, out_shape=jax.ShapeDtypeStruct((n_xs, d), F32),
        compiler_params=_params(("arbitrary",)), name="experts",
    )(route["item_expert"], route["item_block"], route["item_lo"], route["item_hi"], route["item_flags"],
      route["n_items"], xs, w_up, b_up, w_down, b_down)


def _route(counts, before, *, t, bm):
    experts = jnp.arange(N_EXPERTS, dtype=I32)
    cnt = counts.astype(I32)
    bef = before.astype(I32)
    nt = bef.shape[0]
    local = jnp.concatenate([bef[1:], cnt[None]], axis=0) - bef
    ngran = (local + GRANULE - 1) // GRANULE
    lpad = ngran * GRANULE
    lend = jnp.cumsum(lpad, axis=1)
    lstart = lend - lpad
    tot = jnp.sum(lpad, axis=0)
    end = jnp.cumsum(tot)
    start = end - tot
    gseg = start[None] + jnp.cumsum(lpad, axis=0) - lpad
    max_rows = t * TOP_K + nt * N_EXPERTS * (GRANULE - 1)
    n_blocks = -(-max_rows // bm) + 1
    n_xs = n_blocks * bm
    n_work = n_blocks + N_EXPERTS - 1
    first_blk = start // bm
    n_touch = jnp.where(tot > 0, (end - 1) // bm - first_blk + 1, 0)
    wend = jnp.cumsum(n_touch)
    wstart = wend - n_touch
    n_items = wend[-1]
    w_all = jnp.arange(n_work, dtype=I32)
    w = jnp.minimum(w_all, n_items - 1)
    e = jnp.minimum(jnp.sum((w[:, None] >= wend[None, :]).astype(I32), axis=1), N_EXPERTS - 1)
    onehot = (e[:, None] == experts[None, :]).astype(I32)
    pick = lambda v: jnp.sum(onehot * v[None, :], axis=1)
    blk = pick(first_blk) + (w - pick(wstart))
    blk = jnp.minimum(blk + (w_all - w), n_blocks - 1)
    lo = jnp.maximum(pick(start) - blk * bm, 0)
    hi = jnp.minimum(pick(end) - blk * bm, bm)
    prev = lambda v: jnp.concatenate([jnp.full((1,), -1, I32), v[:-1]])
    flags = (blk != prev(blk)).astype(I32) + 2 * (e != prev(e)).astype(I32)
    lanes = lambda v: jnp.pad(v.astype(F32), ((0, 0), (0, LANES - N_EXPERTS)))
    return dict(item_expert=e, item_block=blk, item_lo=lo, item_hi=hi, item_flags=flags,
                n_items=n_items.reshape(1), gseg=gseg.reshape(-1), lstart=lstart.reshape(-1),
                ngran=ngran.reshape(-1), total=end[-1:], n_xs=n_xs,
                lstart_v=jnp.broadcast_to(lanes(lstart)[:, None, :], (nt, SUBLANES, LANES)),
                gseg_v=jnp.broadcast_to(lanes(gseg)[:, None, :], (nt, SUBLANES, LANES)))


def _tail_kernel(slot_ref, slotn_ref, x1_ref, ys_ref, gate_ref, p_ref, gple_ref, wg_ref, wp_ref, gfin_ref, o_ref,
                 ybuf0, ybuf1, sem):
    i = pl.program_id(0)
    last = pl.num_programs(0) - 1
    tm = x1_ref.shape[0]

    def row_copy(rows_ref, buf, b, k, r):
        return pltpu.make_async_copy(ys_ref.at[pl.ds(rows_ref[0, 0, k * tm + r], 1)], buf.at[k, pl.ds(r, 1)],
                                     sem.at[b])

    def wait_rows(buf, b):
        for k in range(TOP_K):
            pltpu.make_async_copy(ys_ref.at[pl.ds(0, tm)], buf.at[k], sem.at[b]).wait()

    @pl.when(i == 0)
    def _():
        for k in range(TOP_K):
            lax.fori_loop(0, tm, lambda r, c, k=k: (row_copy(slot_ref, ybuf0, 0, k, r).start(), c)[1], 0, unroll=8)

    def step(cur, nxt, bc, bn):
        wait_rows(cur, bc)
        n = 0
        for k in range(TOP_K):
            for r in range(tm):
                row_copy(slotn_ref, nxt, bn, k, r).start(priority=n % 2)
                n += 1
        gates = gate_ref[...]
        x2 = x1_ref[...]
        for k in range(TOP_K):
            x2 = x2 + gates[:, k:k + 1] * cur[k]
        g = jax.nn.sigmoid(_dot(_rms(x2, gple_ref[...]).astype(BF16), wg_ref[...]))
        x3 = x2 + _dot(p_ref[...].astype(BF16), wp_ref[...]) * g
        o_ref[...] = _rms(x3, gfin_ref[...])

        @pl.when(i == last)
        def _():
            wait_rows(nxt, bn)

    @pl.when(i % 2 == 0)
    def _():
        step(ybuf0, ybuf1, 0, 1)

    @pl.when(i % 2 == 1)
    def _():
        step(ybuf1, ybuf0, 1, 0)


def _tail(slots, x1, ys, gates, p, g_ple, w_gate, w_proj, g_final, *, tm):
    t, d = x1.shape
    nt = t // tm
    rowsp = lambda w: pl.BlockSpec((tm, w), lambda i: (i, 0))
    full = lambda a: pl.BlockSpec(a.shape, lambda i: (0, 0))
    smem = lambda f: pl.BlockSpec((1, 1, TOP_K * tm), f, memory_space=pltpu.SMEM)
    return pl.pallas_call(
        _tail_kernel, grid=(nt,),
        in_specs=[smem(lambda i: (i, 0, 0)), smem(lambda i: (jnp.minimum(i + 1, nt - 1), 0, 0)), rowsp(d),
                  pl.BlockSpec(memory_space=pl.ANY), rowsp(LANES), rowsp(p.shape[1])]
        + [full(a) for a in (g_ple, w_gate, w_proj, g_final)],
        out_specs=rowsp(d), out_shape=jax.ShapeDtypeStruct((t, d), F32),
        scratch_shapes=[pltpu.VMEM((TOP_K, tm, d), F32), pltpu.VMEM((TOP_K, tm, d), F32),
                        pltpu.SemaphoreType.DMA((2,))],
        compiler_params=_params(("arbitrary",)), name="tail",
    )(slots, slots, x1, ys, gates, p, g_ple, w_gate, w_proj, g_final)


def _block_diag(a, n):
    j, _, r, c = a.shape
    eye = jnp.eye(n, dtype=a.dtype)
    return jnp.einsum("ab,jarc->jarbc", eye, a).reshape(j, n * r, n * c)


def _mixer_tail(groups, lw, *, bm):
    cnt = jnp.zeros((SUBLANES, LANES), F32)
    routed = []
    for x_tok, attn, ssm_n, p_tok, tm in groups:
        tm = min(tm, x_tok.shape[0])
        x1, hm, idx, gates, rank, before, cnt = _outproj(
            x_tok, attn, ssm_n, lw["g_attn"], lw["wo_a"], lw["wo_s"], lw["g_ffn"], lw["wr_pad"], lw["br_pad"], cnt,
            tm=tm)
        routed.append((x1, hm, idx, gates, rank, before, p_tok, tm))
    before_all = jnp.concatenate([r[5][:, 0, :N_EXPERTS] for r in routed], axis=0)
    route = _route(cnt[0, :N_EXPERTS], before_all, t=sum(r[0].shape[0] for r in routed), bm=bm)
    xs, tile0, slots = None, 0, []
    for n, (x1, hm, idx, gates, rank, before, p_tok, tm) in enumerate(routed):
        nt = x1.shape[0] // tm
        xs, slot = _dispatch(route, tile0, hm, idx, rank, before, xs, tm=tm, fill_tail=n == len(routed) - 1)
        slots.append(jnp.transpose(slot[:, :TOP_K].reshape(nt, tm, TOP_K), (0, 2, 1)).reshape(nt, 1, TOP_K * tm))
        tile0 += nt
    ys = _experts(route, xs, lw["w_up"], lw["b_up"], lw["w_down"], lw["b_down"], bm=bm)
    return [_tail(s, x1, ys, gates, p_tok, lw["g_ple"], lw["w_ple_gate"], lw["w_ple_proj"], lw["g_final"], tm=tm)
            for s, (x1, hm, idx, gates, rank, before, p_tok, tm) in zip(slots, routed)]


def kernel(x_prompt, x_sample, p_prompt, p_sample, cache_k, cache_v, cache_logf, state_ssm_re, state_ssm_im,
           page_table, g_mix, w_in, b_fgate, lam_re, lam_im, log_step, b_ssm_re, b_ssm_im, c_ssm_re, c_ssm_im,
           d_ssm, w_glu, b_glu, g_attn_out, g_ssm_out, w_out, g_ffn, w_router, b_router, w_up, b_up, w_down,
           b_down, g_ple, w_ple_gate, w_ple_proj, g_final):
    depth = w_in.shape[0]
    assert depth == 1
    bp, sp, d = x_prompt.shape
    bd, sd, _ = x_sample.shape
    n_heads = b_fgate.shape[1]
    aw = n_heads * HEAD_DIM
    n_groups, n_state = lam_re.shape[1:]
    sw = n_groups * SSM_GROUP
    half = n_groups * n_state
    assert sw == SSM_LANE_BLOCKS * LANES and w_in.shape[2] == 3 * aw + n_heads + sw

    wi = w_in[0]
    wq, wk, wv = wi[:, :aw], wi[:, aw:2 * aw], wi[:, 2 * aw:3 * aw]
    wf, wu = wi[:, 3 * aw:3 * aw + n_heads], wi[:, 3 * aw + n_heads:]
    w_all = jnp.concatenate([wq, wk, wv, wu, jnp.pad(wf, ((0, 0), (0, LANES - n_heads)))], axis=1).astype(BF16)
    wft = jnp.pad(wf.T, ((0, 2 * SUBLANES - n_heads), (0, 0))).astype(BF16)
    bf_col = jnp.pad(b_fgate[0][:, None], ((0, 2 * SUBLANES - n_heads), (0, 0)))
    bf_row = jnp.pad(b_fgate, ((0, 0), (0, LANES - n_heads)))

    abar_re, abar_im, bbr_t, bbi_t = _s5_params(
        lam_re[0], lam_im[0], log_step[0], jnp.transpose(b_ssm_re[0], (0, 2, 1)), jnp.transpose(b_ssm_im[0], (0, 2, 1)))
    gpb = n_groups // SSM_LANE_BLOCKS
    blk4 = lambda a: a.reshape((SSM_LANE_BLOCKS, gpb) + a.shape[1:])
    bd_mat = jnp.concatenate([_block_diag(blk4(bbr_t), gpb), _block_diag(blk4(bbi_t), gpb)], axis=-1).astype(BF16)
    cdr = _block_diag(blk4(jnp.transpose(c_ssm_re[0], (0, 2, 1))), gpb).astype(BF16)
    cdi = _block_diag(blk4(jnp.transpose(c_ssm_im[0], (0, 2, 1))), gpb).astype(BF16)
    ssm_w = (abar_re.reshape(1, half), abar_im.reshape(1, half), bd_mat, cdr, cdi, d_ssm[0].reshape(1, sw),
             w_glu[0].astype(BF16), b_glu, g_ssm_out)

    lw = dict(g_attn=g_attn_out, wo_a=w_out[0, :aw].astype(BF16), wo_s=w_out[0, aw:].astype(BF16), g_ffn=g_ffn,
              wr_pad=jnp.pad(w_router[0], ((0, 0), (0, LANES - N_EXPERTS))).astype(BF16),
              br_pad=jnp.pad(b_router, ((0, 0), (0, LANES - N_EXPERTS))),
              w_up=w_up[0], b_up=b_up[0][:, None, :], w_down=w_down[0],
              b_down=b_down[0][:, None, :], g_ple=g_ple, w_ple_gate=w_ple_gate[0].astype(BF16),
              w_ple_proj=w_ple_proj[0].astype(BF16), g_final=g_final.reshape(1, d))

    q, k, v, kb, vb, u, lft, ct = _inproj_prompt(x_prompt, g_mix, w_all, wft, bf_col, aw=aw, n_heads=n_heads, tm=512)
    attn = _fox_prompt(q, kb, vb, ct, tq=512)
    n_time = min(64, sp)
    zeros_h = jnp.zeros((bp, half), F32)
    ssm_tb, hr_p, hi_p = _ssm(jnp.transpose(u, (1, 0, 2)).reshape(sp * bp, sw), zeros_h, zeros_h, *ssm_w,
                              n_batch=bp, n_time=n_time)
    ssm_p = jnp.transpose(ssm_tb.reshape(sp, bp, sw), (1, 0, 2)).reshape(bp * sp, sw)

    ts = bd * sd
    qs, ks, vs, us, lfs = _inproj_sample(x_sample.reshape(ts, d), g_mix, w_all, bf_row, aw=aw)
    lfs = lfs[:, :n_heads]
    kt_cache = jnp.transpose(cache_k[0], (0, 2, 3, 1))
    vt_cache = jnp.transpose(cache_v[0], (0, 2, 3, 1))
    lft_cache = jnp.transpose(cache_logf[0], (0, 2, 1))
    attn_s = _fox_decode(page_table, qs.reshape(bd, sd, aw), ks.reshape(bd, sd, aw), vs.reshape(bd, sd, aw),
                         jnp.transpose(lfs.reshape(bd, sd, n_heads), (0, 2, 1)), kt_cache, vt_cache, lft_cache,
                         n_group=32)
    ssm_s_tb, hr_s, hi_s = _ssm(jnp.transpose(us.reshape(bd, sd, sw), (1, 0, 2)).reshape(ts, sw),
                                state_ssm_re[0].reshape(bd, half), state_ssm_im[0].reshape(bd, half), *ssm_w,
                                n_batch=bd, n_time=sd)
    ssm_s = jnp.transpose(ssm_s_tb.reshape(sd, bd, sw), (1, 0, 2)).reshape(ts, sw)

    (y_p,) = _mixer_tail(
        [(x_prompt.reshape(bp * sp, d), attn.reshape(bp * sp, aw), ssm_p, p_prompt[0].reshape(bp * sp, -1), 512)],
        lw, bm=512)
    (y_s,) = _mixer_tail(
        [(x_sample.reshape(ts, d), attn_s.reshape(ts, aw), ssm_s, p_sample[0].reshape(ts, -1), 128)], lw, bm=128)

    heads = lambda a, b_, s_: a.reshape(1, b_, s_, n_heads, HEAD_DIM)
    state = lambda a, b_: a.reshape(1, b_, n_groups, n_state)
    return (y_p.reshape(bp, sp, d), y_s.reshape(bd, sd, d),
            heads(k, bp, sp), heads(v, bp, sp), jnp.transpose(lft, (0, 2, 1))[None],
            state(hr_p, bp), state(hi_p, bp),
            heads(ks, bd, sd), heads(vs, bd, sd), lfs.reshape(1, bd, sd, n_heads),
            state(hr_s, bd), state(hi_s, bd))
```

```python
import functools

import jax
import jax.numpy as jnp
from jax import lax
from jax.experimental import pallas as pl
from jax.experimental.pallas import tpu as pltpu

F32 = jnp.float32
BF16 = jnp.bfloat16
I32 = jnp.int32

RMS_EPS = 1e-6
LOG2E = 1.4426950408889634
HEAD_DIM = 64
HEADS_PER_LANE_TILE = 2
LANES = 128
SUBLANES = 8
SSM_GROUP = 16
SSM_STATE = 64
SSM_LANE_BLOCKS = 4
N_EXPERTS = 32
TOP_K = 4
GRANULE = 8
SWIGLU_ALPHA = 1.702
SWIGLU_LIMIT = 7.0
VMEM_LIMIT = 56 * 1024 * 1024


def _params(sem, vmem=VMEM_LIMIT):
    return pltpu.CompilerParams(dimension_semantics=sem, vmem_limit_bytes=vmem)


def _rms(x, g):
    return x * lax.rsqrt(jnp.mean(x * x, axis=-1, keepdims=True) + RMS_EPS) * g


def _log_sigmoid(x):
    return jnp.minimum(x, 0.0) - jnp.log1p(jnp.exp(-jnp.abs(x)))


def _split3(x):
    hi = x.astype(BF16)
    r1 = x - hi.astype(F32)
    mid = r1.astype(BF16)
    lo = (r1 - mid.astype(F32)).astype(BF16)
    return hi, mid, lo


def _dot(a, b):
    return jnp.dot(a, b, preferred_element_type=F32)


def _dot_nt(a, b):
    return lax.dot_general(a, b, (((1,), (1,)), ((), ())), preferred_element_type=F32)


def _dot3(x, m):
    hi, mid, lo = _split3(x)
    return _dot(hi, m) + _dot(mid, m) + _dot(lo, m)


def _s5_params_kernel(lr_ref, li_ref, ls_ref, brt_ref, bit_ref, ar_ref, ai_ref, bbr_ref, bbi_ref):
    lr, li = lr_ref[...], li_ref[...]
    step = jnp.exp(ls_ref[...])
    mag = jnp.exp(lr * step)
    a_re, a_im = mag * jnp.cos(li * step), mag * jnp.sin(li * step)
    nr, ni = a_re - 1.0, a_im
    den = lr * lr + li * li
    coef_re = (nr * lr + ni * li) / den
    coef_im = (ni * lr - nr * li) / den
    ar_ref[...] = a_re
    ai_ref[...] = a_im
    br, bi = brt_ref[...], bit_ref[...]
    bbr_ref[...] = coef_re * br - coef_im * bi
    bbi_ref[...] = coef_re * bi + coef_im * br


def _s5_params(lam_re, lam_im, log_step, b_re_t, b_im_t):
    g, p = lam_re.shape
    c = b_re_t.shape[1]
    return pl.pallas_call(
        _s5_params_kernel,
        out_shape=(jax.ShapeDtypeStruct((g, 1, p), F32), jax.ShapeDtypeStruct((g, 1, p), F32),
                   jax.ShapeDtypeStruct((g, c, p), F32), jax.ShapeDtypeStruct((g, c, p), F32)),
        name="s5_params",
    )(lam_re.reshape(g, 1, p), lam_im.reshape(g, 1, p), log_step.reshape(g, 1, 1), b_re_t, b_im_t)


def _inproj_core(x, g_ref, w_ref, aw):
    h = _rms(x, g_ref[...]).astype(BF16)
    z = _dot(h, w_ref[...])
    return h, z


def _inproj_prompt_kernel(x_ref, g_ref, w_ref, wft_ref, bf_ref, utri_ref,
                          q_ref, k_ref, v_ref, kb_ref, vb_ref, u_ref, lft_ref, ct_ref, carry_ref, *, aw, scale):
    tm = x_ref.shape[1]
    h, z = _inproj_core(x_ref[0], g_ref, w_ref, aw)
    q_ref[0] = (z[:, :aw] * scale).astype(BF16)
    k = z[:, aw:2 * aw]
    v = z[:, 2 * aw:3 * aw]
    k_ref[0] = k
    v_ref[0] = v
    kb_ref[0] = k.astype(BF16)
    vb_ref[0] = v.astype(BF16)
    u_ref[0] = z[:, 3 * aw:4 * aw]
    ft = _dot_nt(wft_ref[...], h) + bf_ref[...]
    lft = _log_sigmoid(ft)
    n_heads = lft_ref.shape[1]
    lft_ref[0] = lft[:n_heads]

    @pl.when(pl.program_id(1) == 0)
    def _():
        carry_ref[...] = jnp.zeros_like(carry_ref)

    cs = _dot3(lft, utri_ref[...]) + carry_ref[:, :1]
    carry_ref[...] = jnp.broadcast_to(cs[:, tm - 1:tm], carry_ref.shape)
    ct_ref[0] = cs[:n_heads]


def _inproj_prompt(x, g_mix, w_all, wft, bf_col, *, aw, n_heads, tm):
    b, s, d = x.shape
    tm = min(tm, s)
    assert s % tm == 0
    ii = lax.broadcasted_iota(I32, (tm, tm), 0)
    jj = lax.broadcasted_iota(I32, (tm, tm), 1)
    utri = (ii <= jj).astype(BF16)
    row = lambda bi, si: (bi, si, 0)
    col = lambda bi, si: (bi, 0, si)
    const = lambda bi, si: (0, 0)
    wide = pl.BlockSpec((1, tm, aw), row)
    tall = pl.BlockSpec((1, n_heads, tm), col)
    return pl.pallas_call(
        functools.partial(_inproj_prompt_kernel, aw=aw, scale=HEAD_DIM ** -0.5 * LOG2E),
        grid=(b, s // tm),
        in_specs=[pl.BlockSpec((1, tm, d), row), pl.BlockSpec((1, d), const), pl.BlockSpec(w_all.shape, const),
                  pl.BlockSpec(wft.shape, const), pl.BlockSpec(bf_col.shape, const), pl.BlockSpec((tm, tm), const)],
        out_specs=[wide, wide, wide, wide, wide, wide, tall, tall],
        out_shape=[jax.ShapeDtypeStruct((b, s, aw), BF16), jax.ShapeDtypeStruct((b, s, aw), F32),
                   jax.ShapeDtypeStruct((b, s, aw), F32), jax.ShapeDtypeStruct((b, s, aw), BF16),
                   jax.ShapeDtypeStruct((b, s, aw), BF16), jax.ShapeDtypeStruct((b, s, aw), F32),
                   jax.ShapeDtypeStruct((b, n_heads, s), F32), jax.ShapeDtypeStruct((b, n_heads, s), F32)],
        scratch_shapes=[pltpu.VMEM((2 * SUBLANES, LANES), F32)],
        compiler_params=_params(("arbitrary", "arbitrary")),
        name="inproj_prompt",
    )(x, g_mix, w_all, wft, bf_col, utri)


def _inproj_sample_kernel(x_ref, g_ref, w_ref, bf_ref, q_ref, k_ref, v_ref, u_ref, lf_ref, *, aw, scale):
    _, z = _inproj_core(x_ref[...], g_ref, w_ref, aw)
    q_ref[...] = (z[:, :aw] * scale).astype(BF16)
    k_ref[...] = z[:, aw:2 * aw]
    v_ref[...] = z[:, 2 * aw:3 * aw]
    u_ref[...] = z[:, 3 * aw:4 * aw]
    lf_ref[...] = _log_sigmoid(z[:, 4 * aw:] + bf_ref[...])


def _inproj_sample(x, g_mix, w_all, bf_row, *, aw):
    t, d = x.shape
    outs = [jax.ShapeDtypeStruct((t, aw), BF16)] + [jax.ShapeDtypeStruct((t, aw), F32)] * 3
    outs.append(jax.ShapeDtypeStruct((t, LANES), F32))
    return pl.pallas_call(
        functools.partial(_inproj_sample_kernel, aw=aw, scale=HEAD_DIM ** -0.5),
        out_shape=outs, compiler_params=_params(None), name="inproj_sample",
    )(x, g_mix, w_all, bf_row)


_L_ONE, _L_BIAS = HEAD_DIM, HEAD_DIM + 3


def _bias_rows(c_row, piece_row0, ones_row0, n_ones, width):
    hi, mid, lo = (p.astype(F32) for p in _split3(c_row))
    r = lax.broadcasted_iota(I32, (LANES, width), 0)
    out = jnp.where((r >= ones_row0) & (r < ones_row0 + n_ones), 1.0, 0.0)
    for i, piece in enumerate((hi, mid, lo)):
        out = jnp.where(r == piece_row0 + i, piece, out)
    return out


def _fox_prompt_kernel(q_ref, k_ref, v_ref, ct_ref, o_ref, kaug, vaug, *, tq):
    qi = pl.program_id(2)
    pair = pl.program_id(1)
    n_kv = k_ref.shape[1] // tq
    lane = lax.broadcasted_iota(I32, (tq, LANES), 1)
    row = lax.broadcasted_iota(I32, (tq, tq), 0)
    colm = lax.broadcasted_iota(I32, (tq, tq), 1)
    src = lax.broadcasted_iota(I32, (LANES, LANES), 0)
    dst = lax.broadcasted_iota(I32, (LANES, LANES), 1)
    sels = [((src == dst + hh * HEAD_DIM) & (dst < HEAD_DIM)).astype(BF16) for hh in range(HEADS_PER_LANE_TILE)]

    @pl.when(qi == 0)
    def _():
        for hh in range(HEADS_PER_LANE_TILE):
            head = pair * HEADS_PER_LANE_TILE + hh
            for c in range(n_kv):
                rows = slice(c * tq, (c + 1) * tq)
                ck = ct_ref[0, head, c:c + 1, :] * LOG2E
                kb = _bias_rows(-ck, _L_ONE, _L_BIAS, 3, tq).T
                kaug[hh, rows, :] = (_dot(k_ref[0, rows, :], sels[hh]) + kb).astype(BF16)
                vaug[hh, rows, :] = (_dot(v_ref[0, rows, :], sels[hh])
                                     + jnp.where(lane == _L_ONE, 1.0, 0.0)).astype(BF16)

    qas = []
    for hh in range(HEADS_PER_LANE_TILE):
        cq = ct_ref[0, pair * HEADS_PER_LANE_TILE + hh, pl.ds(qi, 1), :] * LOG2E
        qas.append((_dot(q_ref[0], sels[hh]) + _bias_rows(cq, _L_BIAS, _L_ONE, 3, tq).T).astype(BF16))

    def tile(j, carry, masked):
        start = pl.multiple_of(j * tq, tq)
        new = []
        for hh in range(HEADS_PER_LANE_TILE):
            m, acc = carry[hh]
            ka = kaug[hh, pl.ds(start, tq), :]
            va = vaug[hh, pl.ds(start, tq), :]
            s = _dot_nt(qas[hh], ka)
            if masked:
                s = jnp.where(colm <= row, s, -jnp.inf)
            m_new = jnp.maximum(m, jnp.max(s, axis=-1, keepdims=True))
            p = jnp.exp2(s - m_new).astype(BF16)
            acc = jnp.exp2(m - m_new) * acc + _dot(p, va)
            new.append((m_new, acc))
        return tuple(new)

    init = tuple((jnp.full((tq, 1), -jnp.inf, F32), jnp.zeros((tq, LANES), F32))
                 for _ in range(HEADS_PER_LANE_TILE))
    carry = lax.fori_loop(0, qi // 2, lambda jj, c: tile(2 * jj + 1, tile(2 * jj, c, False), False), init)
    carry = lax.cond(qi % 2 == 1, lambda c: tile(qi - 1, c, False), lambda c: c, carry)
    (_, acc0), (_, acc1) = tile(qi, carry, True)
    out0 = acc0 / acc0[:, _L_ONE:_L_ONE + 1]
    out1 = acc1 / acc1[:, _L_ONE:_L_ONE + 1]
    o_ref[0] = jnp.where(lane < HEAD_DIM, out0, pltpu.roll(out1, HEAD_DIM, 1))


def _fox_prompt(q, kb, vb, ct, *, tq):
    b, s, aw = q.shape
    tq = min(tq, s)
    assert s % tq == 0
    n_pairs = aw // LANES
    ct = ct.reshape(b, ct.shape[1], s // tq, tq)
    return pl.pallas_call(
        functools.partial(_fox_prompt_kernel, tq=tq),
        grid=(b, n_pairs, s // tq),
        in_specs=[pl.BlockSpec((1, tq, LANES), lambda bi, p, qi: (bi, qi, p)),
                  pl.BlockSpec((1, s, LANES), lambda bi, p, qi: (bi, 0, p)),
                  pl.BlockSpec((1, s, LANES), lambda bi, p, qi: (bi, 0, p)),
                  pl.BlockSpec((1,) + ct.shape[1:], lambda bi, p, qi: (bi, 0, 0, 0))],
        out_specs=pl.BlockSpec((1, tq, LANES), lambda bi, p, qi: (bi, qi, p)),
        out_shape=jax.ShapeDtypeStruct((b, s, aw), F32),
        scratch_shapes=[pltpu.VMEM((HEADS_PER_LANE_TILE, s, LANES), BF16),
                        pltpu.VMEM((HEADS_PER_LANE_TILE, s, LANES), BF16)],
        compiler_params=_params(("arbitrary", "arbitrary", "arbitrary")),
        name="fox_prompt",
    )(q, kb, vb, ct)


def _fox_decode_kernel(pt_ref, q_ref, kn_ref, vn_ref, lfn_ref, ustrict_ref, *rest, n_group, n_new):
    kt_refs = rest[:n_group]
    vt_refs = rest[n_group:2 * n_group]
    lf_refs = rest[2 * n_group:3 * n_group]
    o_ref, m_ref, l_ref, acc_ref, tail_ref = rest[3 * n_group:]
    j = pl.program_id(1)
    n_heads = lfn_ref.shape[1]
    width = q_ref.shape[2]
    rows = n_new * n_heads

    lane_head = lax.broadcasted_iota(I32, (n_heads, width), 1) // HEAD_DIM
    sub = lax.broadcasted_iota(I32, (n_heads, width), 0)
    hmask = lane_head == sub
    q = q_ref[0].astype(F32)
    qexp = jnp.concatenate([jnp.where(hmask, jnp.broadcast_to(q[t:t + 1], (n_heads, width)), 0.0)
                            for t in range(n_new)], axis=0)
    trow = lax.broadcasted_iota(I32, (rows, 1), 0) // n_heads

    @pl.when(j == 0)
    def _():
        lfn = lfn_ref[0]
        kn, vn = kn_ref[0], vn_ref[0]
        run = jnp.zeros((n_heads, 1), F32)
        logits = []
        for sp in range(n_new):
            run = run + lfn[:, sp:sp + 1]
            bias = jnp.concatenate([run] * n_new, axis=0)
            sc = jnp.sum(qexp * kn[sp:sp + 1], axis=-1, keepdims=True) - bias
            logits.append(jnp.where(trow >= sp, sc, -jnp.inf))
        m = logits[0]
        for sc in logits[1:]:
            m = jnp.maximum(m, sc)
        l = jnp.zeros((rows, 1), F32)
        acc = jnp.zeros((rows, width), F32)
        for sp in range(n_new):
            p = jnp.exp(logits[sp] - m)
            l = l + p
            acc = acc + p * vn[sp:sp + 1]
        m_ref[...] = m
        l_ref[...] = l
        acc_ref[...] = acc
        tail_ref[...] = jnp.zeros_like(tail_ref)

    qb = qexp.astype(BF16)
    tail = tail_ref[:, :1]
    scores = []
    for g in range(n_group):
        kt = kt_refs[g][0].reshape(width, -1).astype(BF16)
        lf = lf_refs[g][0]
        lf_pad = jnp.concatenate([lf, jnp.zeros_like(lf)], axis=0)
        suffix = _dot3(lf_pad, ustrict_ref[...])[:n_heads] + tail
        tail = tail + jnp.sum(lf, axis=-1, keepdims=True)
        scores.append(_dot(qb, kt) + jnp.concatenate([suffix] * n_new, axis=0))
    tail_ref[...] = jnp.broadcast_to(tail, tail_ref.shape)
    m_old = m_ref[...]
    m_new = m_old
    for s in scores:
        m_new = jnp.maximum(m_new, jnp.max(s, axis=-1, keepdims=True))
    alpha = jnp.exp(m_old - m_new)
    l = alpha * l_ref[...]
    acc = alpha * acc_ref[...]
    for g, s in enumerate(scores):
        p = jnp.exp(s - m_new)
        l = l + jnp.sum(p, axis=-1, keepdims=True)
        acc = acc + _dot_nt(p.astype(BF16), vt_refs[g][0].reshape(width, -1).astype(BF16))
    l_ref[...] = l
    acc_ref[...] = acc
    m_ref[...] = m_new

    @pl.when(j == pl.num_programs(1) - 1)
    def _():
        o = acc_ref[...] / l_ref[...]
        outs = [jnp.sum(jnp.where(hmask, o[t * n_heads:(t + 1) * n_heads], 0.0), axis=0, keepdims=True)
                for t in range(n_new)]
        o_ref[0] = jnp.concatenate(outs, axis=0)


def _fox_decode(page_table, q, k_new, v_new, lfn_t, kt_cache, vt_cache, lft_cache, *, n_group):
    bd, n_new, width = q.shape
    n_pages = page_table.shape[1]
    n_heads, page = lft_cache.shape[1:]
    n_group = min(n_group, n_pages)
    assert n_pages % n_group == 0
    n_steps = n_pages // n_group
    ii = lax.broadcasted_iota(I32, (page, page), 0)
    jj = lax.broadcasted_iota(I32, (page, page), 1)
    ustrict = (ii > jj).astype(BF16)

    def page_map(g, ndim):
        def index_map(b, j, pt):
            return (pt[b, n_pages - 1 - (j * n_group + g)],) + (0,) * ndim
        return index_map

    per_batch = lambda b, j, pt: (b, 0, 0)
    in_specs = [pl.BlockSpec((1, n_new, width), per_batch), pl.BlockSpec((1, n_new, width), per_batch),
                pl.BlockSpec((1, n_new, width), per_batch), pl.BlockSpec((1, n_heads, n_new), per_batch),
                pl.BlockSpec((page, page), lambda b, j, pt: (0, 0))]
    in_specs += [pl.BlockSpec((1,) + kt_cache.shape[1:], page_map(g, 3)) for g in range(n_group)]
    in_specs += [pl.BlockSpec((1,) + vt_cache.shape[1:], page_map(g, 3)) for g in range(n_group)]
    in_specs += [pl.BlockSpec((1, n_heads, page), page_map(g, 2)) for g in range(n_group)]
    rows = n_new * n_heads
    grid_spec = pltpu.PrefetchScalarGridSpec(
        num_scalar_prefetch=1, grid=(bd, n_steps), in_specs=in_specs,
        out_specs=pl.BlockSpec((1, n_new, width), per_batch),
        scratch_shapes=[pltpu.VMEM((rows, 1), F32), pltpu.VMEM((rows, 1), F32), pltpu.VMEM((rows, width), F32),
                        pltpu.VMEM((n_heads, LANES), F32)])
    return pl.pallas_call(
        functools.partial(_fox_decode_kernel, n_group=n_group, n_new=n_new),
        grid_spec=grid_spec, out_shape=jax.ShapeDtypeStruct((bd, n_new, width), F32),
        compiler_params=_params(("arbitrary", "arbitrary")), name="fox_decode",
    )(page_table, q, k_new, v_new, lfn_t, ustrict,
      *([kt_cache] * n_group), *([vt_cache] * n_group), *([lft_cache] * n_group))


def _ssm_kernel(u_ref, h0r_ref, h0i_ref, ar_ref, ai_ref, bd_ref, cdr_ref, cdi_ref, d_ref, wglu_ref, bglu_ref,
                gout_ref, o_ref, hr_ref, hi_ref, bu_ref, h_ref, *, n_batch, n_time):
    i = pl.program_id(0)
    half = ar_ref.shape[1]
    blk = half // SSM_LANE_BLOCKS
    rows = n_batch * n_time

    @pl.when(i == 0)
    def _():
        h_ref[:, :half] = h0r_ref[...]
        h_ref[:, half:] = h0i_ref[...]

    u = u_ref[...]
    ub = u.astype(BF16)
    ys = []
    for jb in range(SSM_LANE_BLOCKS):
        lo_r, lo_i = jb * blk, half + jb * blk
        bu = _dot(ub[:, jb * LANES:(jb + 1) * LANES], bd_ref[jb])
        bu_ref[:, lo_r:lo_r + blk] = bu[:, :blk]
        bu_ref[:, lo_i:lo_i + blk] = bu[:, blk:]
        a_r = jnp.broadcast_to(ar_ref[:, lo_r:lo_r + blk], (SUBLANES, blk))
        a_i = jnp.broadcast_to(ai_ref[:, lo_r:lo_r + blk], (SUBLANES, blk))
        for bg in range(n_batch // SUBLANES):
            b0 = bg * SUBLANES
            h_r = h_ref[b0:b0 + SUBLANES, lo_r:lo_r + blk]
            h_i = h_ref[b0:b0 + SUBLANES, lo_i:lo_i + blk]
            for t in range(n_time):
                r0 = t * n_batch + b0
                n_r = a_r * h_r - a_i * h_i + bu_ref[r0:r0 + SUBLANES, lo_r:lo_r + blk]
                n_i = a_r * h_i + a_i * h_r + bu_ref[r0:r0 + SUBLANES, lo_i:lo_i + blk]
                bu_ref[r0:r0 + SUBLANES, lo_r:lo_r + blk] = n_r
                bu_ref[r0:r0 + SUBLANES, lo_i:lo_i + blk] = n_i
                h_r, h_i = n_r, n_i
            h_ref[b0:b0 + SUBLANES, lo_r:lo_r + blk] = h_r
            h_ref[b0:b0 + SUBLANES, lo_i:lo_i + blk] = h_i
        hre = bu_ref[:, lo_r:lo_r + blk].astype(BF16)
        him = bu_ref[:, lo_i:lo_i + blk].astype(BF16)
        ys.append(_dot(hre, cdr_ref[jb]) - _dot(him, cdi_ref[jb]))
    y = jnp.concatenate(ys, axis=-1) + d_ref[...] * u
    y = jax.nn.gelu(y, approximate=True)
    y = y * jax.nn.sigmoid(_dot(y.astype(BF16), wglu_ref[...]) + bglu_ref[...])
    o_ref[...] = _rms(y, gout_ref[...]).astype(BF16)

    @pl.when(i == pl.num_programs(0) - 1)
    def _():
        hr_ref[...] = h_ref[:, :half]
        hi_ref[...] = h_ref[:, half:]


def _ssm(u_tb, h0r, h0i, abar_re, abar_im, bd, cdr, cdi, d_row, w_glu, b_glu, g_out, *, n_batch, n_time):
    total, width = u_tb.shape
    half = abar_re.shape[1]
    rows = n_batch * n_time
    assert total % rows == 0 and n_batch % SUBLANES == 0
    const2 = lambda i: (0, 0)
    const3 = lambda i: (0, 0, 0)
    full = lambda a: pl.BlockSpec(a.shape, const2 if a.ndim == 2 else const3)
    return pl.pallas_call(
        functools.partial(_ssm_kernel, n_batch=n_batch, n_time=n_time),
        grid=(total // rows,),
        in_specs=[pl.BlockSpec((rows, width), lambda i: (i, 0))] + [full(a) for a in (
            h0r, h0i, abar_re, abar_im, bd, cdr, cdi, d_row, w_glu, b_glu, g_out)],
        out_specs=[pl.BlockSpec((rows, width), lambda i: (i, 0)), pl.BlockSpec((n_batch, half), const2),
                   pl.BlockSpec((n_batch, half), const2)],
        out_shape=[jax.ShapeDtypeStruct((total, width), BF16), jax.ShapeDtypeStruct((n_batch, half), F32),
                   jax.ShapeDtypeStruct((n_batch, half), F32)],
        scratch_shapes=[pltpu.VMEM((rows, 2 * half), F32), pltpu.VMEM((n_batch, 2 * half), F32)],
        compiler_params=_params(("arbitrary",)), name="ssm",
    )(u_tb, h0r, h0i, abar_re, abar_im, bd, cdr, cdi, d_row, w_glu, b_glu, g_out)


def _outproj_kernel(x_ref, a_ref, s_ref, ga_ref, woa_ref, wos_ref, gffn_ref, wr_ref, br_ref, lstrict_ref, cnt0_ref,
                    x1_ref, hm_ref, idx_ref, gate_ref, rank_ref, before_ref, cnt_ref, cnt_sc):
    i = pl.program_id(0)
    tm = x_ref.shape[0]

    @pl.when(i == 0)
    def _():
        cnt_sc[...] = cnt0_ref[...]

    before_ref[0] = cnt_sc[...]

    an = _rms(a_ref[...], ga_ref[...]).astype(BF16)
    x1 = x_ref[...] + _dot(an, woa_ref[...]) + _dot(s_ref[...], wos_ref[...])
    x1_ref[...] = x1
    hm = _rms(x1, gffn_ref[...])
    hm_ref[...] = hm
    logits = _dot(hm.astype(BF16), wr_ref[...]) + br_ref[...]
    lane_i = lax.broadcasted_iota(I32, (tm, LANES), 1)
    lane = lane_i.astype(F32)
    work = jnp.where(lane_i < N_EXPERTS, logits, -jnp.inf)
    vals, idxs = [], []
    for _ in range(TOP_K):
        mx = jnp.max(work, axis=-1, keepdims=True)
        ix = jnp.min(jnp.where(work == mx, lane, float(LANES)), axis=-1, keepdims=True)
        vals.append(mx)
        idxs.append(ix)
        work = jnp.where(lane == ix, -jnp.inf, work)
    exps = [jnp.exp(v - vals[0]) for v in vals]
    denom = exps[0]
    for e in exps[1:]:
        denom = denom + e
    onehot = jnp.zeros((tm, LANES), F32)
    for ix in idxs:
        onehot = onehot + (lane == ix).astype(F32)
    before = _dot(lstrict_ref[...], onehot.astype(BF16)) + cnt_sc[:1, :]
    idx_out = jnp.zeros((tm, LANES), I32)
    gate_out = jnp.zeros((tm, LANES), F32)
    rank_out = jnp.zeros((tm, LANES), I32)
    for k in range(TOP_K):
        rk = jnp.sum(jnp.where(lane == idxs[k], before, 0.0), axis=-1, keepdims=True)
        idx_out = jnp.where(lane_i == k, idxs[k].astype(I32), idx_out)
        gate_out = jnp.where(lane_i == k, exps[k] / denom, gate_out)
        rank_out = jnp.where(lane_i == k, rk.astype(I32), rank_out)
    idx_ref[...] = idx_out
    gate_ref[...] = gate_out
    rank_ref[...] = rank_out
    cnt = cnt_sc[:1, :] + jnp.sum(onehot, axis=0, keepdims=True)
    cnt_sc[...] = jnp.broadcast_to(cnt, cnt_sc.shape)
    cnt_ref[...] = jnp.broadcast_to(cnt, cnt_ref.shape)


def _outproj(x, attn, ssm_n, g_attn, wo_a, wo_s, g_ffn, wr_pad, br_pad, cnt0, *, tm):
    t, d = x.shape
    assert t % tm == 0
    ii = lax.broadcasted_iota(I32, (tm, tm), 0)
    jj = lax.broadcasted_iota(I32, (tm, tm), 1)
    lstrict = (jj < ii).astype(BF16)
    rowsp = lambda w: pl.BlockSpec((tm, w), lambda i: (i, 0))
    full = lambda a: pl.BlockSpec(a.shape, lambda i: (0, 0))
    aw = attn.shape[1]
    return pl.pallas_call(
        _outproj_kernel, grid=(t // tm,),
        in_specs=[rowsp(d), rowsp(aw), rowsp(ssm_n.shape[1])] + [full(a) for a in (
            g_attn, wo_a, wo_s, g_ffn, wr_pad, br_pad, lstrict, cnt0)],
        out_specs=[rowsp(d), rowsp(d), rowsp(LANES), rowsp(LANES), rowsp(LANES),
                   pl.BlockSpec((1, SUBLANES, LANES), lambda i: (i, 0, 0)),
                   pl.BlockSpec((SUBLANES, LANES), lambda i: (0, 0))],
        out_shape=[jax.ShapeDtypeStruct((t, d), F32), jax.ShapeDtypeStruct((t, d), F32),
                   jax.ShapeDtypeStruct((t, LANES), I32), jax.ShapeDtypeStruct((t, LANES), F32),
                   jax.ShapeDtypeStruct((t, LANES), I32), jax.ShapeDtypeStruct((t // tm, SUBLANES, LANES), F32),
                   jax.ShapeDtypeStruct((SUBLANES, LANES), F32)],
        scratch_shapes=[pltpu.VMEM((SUBLANES, LANES), F32)],
        compiler_params=_params(("arbitrary",)), name="outproj_router",
    )(x, attn, ssm_n, g_attn, wo_a, wo_s, g_ffn, wr_pad, br_pad, lstrict, cnt0)


def _dispatch_kernel(gseg_ref, lstart_ref, ngran_ref, nrows_ref, total_ref, hm_ref, idx_ref, rank_ref, before_ref,
                     lstartv_ref, gsegv_ref, *rest, chunk, has_prev, fill_tail):
    xs_ref, slot_ref, sorted_ref, zeros_ref, sem = (rest[:1] + rest[2:]) if has_prev else rest
    i = pl.program_id(0)
    tm = hm_ref.shape[0]
    lane = lax.broadcasted_iota(I32, (tm, LANES), 1)
    idx = idx_ref[...]
    rank = rank_ref[...].astype(F32)
    local_off = lstartv_ref[0][:1] - before_ref[0][:1]
    global_off = gsegv_ref[0][:1] - before_ref[0][:1]
    lslots = jnp.full((tm, LANES), -1.0, F32)
    gslots = jnp.zeros((tm, LANES), I32)
    for k in range(TOP_K):
        sel = lane == idx[:, k:k + 1]
        r_k = rank[:, k:k + 1]
        ls = jnp.sum(jnp.where(sel, local_off, 0.0), axis=-1, keepdims=True) + r_k
        gs = jnp.sum(jnp.where(sel, global_off, 0.0), axis=-1, keepdims=True) + r_k
        lslots = jnp.where(lane == k, ls, lslots)
        gslots = jnp.where(lane == k, gs.astype(I32), gslots)
    slot_ref[...] = gslots
    lt = lslots.T
    hb = hm_ref[...].astype(BF16)
    buf = i % 2
    for c in range(sorted_ref.shape[1] // chunk):
        pos = (c * chunk + lax.broadcasted_iota(I32, (chunk, tm), 0)).astype(F32)
        perm = jnp.zeros((chunk, tm), F32)
        for k in range(TOP_K):
            perm = perm + jnp.where(pos == lt[k:k + 1], 1.0, 0.0)
        sorted_ref[buf, c * chunk:(c + 1) * chunk, :] = _dot(perm.astype(BF16), hb)

    def copy(src, dst, rows, b=buf):
        src = pl.multiple_of(src, GRANULE)
        dst = pl.multiple_of(dst, GRANULE)
        return pltpu.make_async_copy(sorted_ref.at[b, pl.ds(src, rows)], xs_ref.at[pl.ds(dst, rows)], sem.at[b])

    big = 4 * GRANULE
    for e in range(N_EXPERTS):
        n = ngran_ref[i * N_EXPERTS + e]
        src0 = lstart_ref[i * N_EXPERTS + e]
        dst0 = gseg_ref[i * N_EXPERTS + e]
        n_big = lax.shift_right_logical(n, 2)
        rest0 = n_big * big

        def body_big(g, c, src0=src0, dst0=dst0):
            copy(src0 + g * big, dst0 + g * big, big).start()
            return c
        lax.fori_loop(0, n_big, body_big, 0)

        def body_small(g, c, src0=src0 + rest0, dst0=dst0 + rest0):
            copy(src0 + g * GRANULE, dst0 + g * GRANULE, GRANULE).start()
            return c
        lax.fori_loop(0, n - n_big * 4, body_small, 0)

    def wait_rows(rows, b):
        unit = 256
        lax.fori_loop(0, lax.shift_right_logical(rows, 8), lambda g, c: (copy(0, 0, unit, b).wait(), c)[1], 0)
        lax.fori_loop(0, lax.shift_right_logical(rows & (unit - 1), 3),
                      lambda g, c: (copy(0, 0, GRANULE, b).wait(), c)[1], 0)

    @pl.when(i > 0)
    def _():
        wait_rows(nrows_ref[i - 1], 1 - buf)

    @pl.when(i == pl.num_programs(0) - 1)
    def _():
        wait_rows(nrows_ref[i], buf)

    if fill_tail:
        @pl.when(i == pl.num_programs(0) - 1)
        def _():
            zeros_ref[...] = jnp.zeros_like(zeros_ref)
            end = total_ref[0]

            def fill(g):
                dst = pl.multiple_of(end + g * GRANULE, GRANULE)
                return pltpu.make_async_copy(zeros_ref, xs_ref.at[pl.ds(dst, GRANULE)], sem.at[buf])

            n_fill = (xs_ref.shape[0] - end) // GRANULE
            lax.fori_loop(0, n_fill, lambda g, c: (fill(g).start(), c)[1], 0)
            lax.fori_loop(0, n_fill, lambda g, c: (fill(g).wait(), c)[1], 0)


def _dispatch(route, tile0, hm, idx, rank, before, xs_prev, *, tm, fill_tail):
    t, d = hm.shape
    nt = t // tm
    n_local = TOP_K * tm + N_EXPERTS * GRANULE
    chunk = 256 if n_local % 256 == 0 else LANES
    assert n_local % chunk == 0
    has_prev = xs_prev is not None
    tiles = slice(tile0, tile0 + nt)
    flat = slice(tile0 * N_EXPERTS, (tile0 + nt) * N_EXPERTS)
    rowsp = lambda w: pl.BlockSpec((tm, w), lambda i, *_: (i, 0))
    tile3 = pl.BlockSpec((1, SUBLANES, LANES), lambda i, *_: (i, 0, 0))
    n_prefetch = 5
    in_specs = [rowsp(d), rowsp(LANES), rowsp(LANES), tile3, tile3, tile3]
    operands = [route["gseg"][flat], route["lstart"][flat], route["ngran"][flat], route["nrows"][tiles],
                route["total"], hm, idx, rank, before, route["lstart_v"][tiles], route["gseg_v"][tiles]]
    aliases = {}
    if has_prev:
        in_specs.append(pl.BlockSpec(memory_space=pl.ANY))
        aliases = {len(operands): 0}
        operands.append(xs_prev)
    grid_spec = pltpu.PrefetchScalarGridSpec(
        num_scalar_prefetch=n_prefetch, grid=(nt,), in_specs=in_specs,
        out_specs=[pl.BlockSpec(memory_space=pl.ANY), rowsp(LANES)],
        scratch_shapes=[pltpu.VMEM((2, n_local, d), F32), pltpu.VMEM((GRANULE, d), F32),
                        pltpu.SemaphoreType.DMA((2,))])
    return pl.pallas_call(
        functools.partial(_dispatch_kernel, chunk=chunk, has_prev=has_prev, fill_tail=fill_tail),
        grid_spec=grid_spec, input_output_aliases=aliases,
        out_shape=[jax.ShapeDtypeStruct((route["n_xs"], d), F32), jax.ShapeDtypeStruct((t, LANES), I32)],
        compiler_params=_params(("arbitrary",)), name="dispatch",
    )(*operands)


def _experts_kernel(ie_ref, ib_ref, lo_ref, hi_ref, fl_ref, ni_ref, xs_ref, wup_ref, bup_ref, wdn_ref, bdn_ref,
                    ys_ref, wup_b, wdn_b, *, d_ff):
    w = pl.program_id(0)
    bm = xs_ref.shape[0]

    @pl.when(w >= ni_ref[0])
    def _():
        ys_ref[...] = jnp.zeros_like(ys_ref)

    @pl.when(w < ni_ref[0])
    def _():
        first = (fl_ref[w] & 1) == 1

        @pl.when((fl_ref[w] & 2) == 2)
        def _():
            step = 128
            for r in range(0, wup_b.shape[0], step):
                wup_b[r:r + step, :] = wup_ref[0, r:r + step, :].astype(BF16)
            for r in range(0, wdn_b.shape[0], step):
                wdn_b[r:r + step, :] = wdn_ref[0, r:r + step, :].astype(BF16)

        x = xs_ref[...].astype(BF16)
        up = _dot(x, wup_b[...]) + bup_ref[0]
        gate = jnp.minimum(up[:, :d_ff], SWIGLU_LIMIT)
        lin = jnp.clip(up[:, d_ff:], -SWIGLU_LIMIT, SWIGLU_LIMIT)
        act = gate * jax.nn.sigmoid(SWIGLU_ALPHA * gate) * (lin + 1.0)
        out = _dot(act.astype(BF16), wdn_b[...]) + bdn_ref[0]

        @pl.when(first)
        def _():
            ys_ref[...] = out

        @pl.when(jnp.logical_not(first))
        def _():
            rowi = lax.broadcasted_iota(I32, (bm, 1), 0)
            mine = (rowi >= lo_ref[w]) & (rowi < hi_ref[w])
            ys_ref[...] = jnp.where(mine, out, ys_ref[...])


def _experts(route, xs, w_up, b_up, w_down, b_down, *, bm):
    n_xs, d = xs.shape
    n_work = route["item_expert"].shape[0]
    d_ff = w_down.shape[1]
    by_block = lambda w, ie, ib, lo, hi, fl, ni: (ib[w], 0)
    by_expert = lambda w, ie, ib, lo, hi, fl, ni: (ie[w], 0, 0)
    grid_spec = pltpu.PrefetchScalarGridSpec(
        num_scalar_prefetch=6, grid=(n_work,),
        in_specs=[pl.BlockSpec((bm, d), by_block),
                  pl.BlockSpec((1,) + w_up.shape[1:], by_expert), pl.BlockSpec((1, 1, b_up.shape[2]), by_expert),
                  pl.BlockSpec((1,) + w_down.shape[1:], by_expert), pl.BlockSpec((1, 1, d), by_expert)],
        out_specs=pl.BlockSpec((bm, d), by_block),
        scratch_shapes=[pltpu.VMEM(w_up.shape[1:], BF16), pltpu.VMEM(w_down.shape[1:], BF16)])
    return pl.pallas_call(
        functools.partial(_experts_kernel, d_ff=d_ff),
        grid_spec=grid_spec, out_shape=jax.ShapeDtypeStruct((n_xs, d), F32),
        compiler_params=_params(("arbitrary",)), name="experts",
    )(route["item_expert"], route["item_block"], route["item_lo"], route["item_hi"], route["item_flags"],
      route["n_items"], xs, w_up, b_up, w_down, b_down)


def _route(counts, before, *, t, bm):
    experts = jnp.arange(N_EXPERTS, dtype=I32)
    cnt = counts.astype(I32)
    bef = before.astype(I32)
    nt = bef.shape[0]
    local = jnp.concatenate([bef[1:], cnt[None]], axis=0) - bef
    ngran = (local + GRANULE - 1) // GRANULE
    lpad = ngran * GRANULE
    lend = jnp.cumsum(lpad, axis=1)
    lstart = lend - lpad
    tot = jnp.sum(lpad, axis=0)
    end = jnp.cumsum(tot)
    start = end - tot
    gseg = start[None] + jnp.cumsum(lpad, axis=0) - lpad
    max_rows = t * TOP_K + nt * N_EXPERTS * (GRANULE - 1)
    n_blocks = -(-max_rows // bm) + 1
    n_xs = n_blocks * bm
    n_work = n_blocks + N_EXPERTS - 1
    first_blk = start // bm
    n_touch = jnp.where(tot > 0, (end - 1) // bm - first_blk + 1, 0)
    wend = jnp.cumsum(n_touch)
    wstart = wend - n_touch
    n_items = wend[-1]
    w_all = jnp.arange(n_work, dtype=I32)
    w = jnp.minimum(w_all, n_items - 1)
    e = jnp.minimum(jnp.sum((w[:, None] >= wend[None, :]).astype(I32), axis=1), N_EXPERTS - 1)
    onehot = (e[:, None] == experts[None, :]).astype(I32)
    pick = lambda v: jnp.sum(onehot * v[None, :], axis=1)
    blk = pick(first_blk) + (w - pick(wstart))
    blk = jnp.minimum(blk + (w_all - w), n_blocks - 1)
    lo = jnp.maximum(pick(start) - blk * bm, 0)
    hi = jnp.minimum(pick(end) - blk * bm, bm)
    prev = lambda v: jnp.concatenate([jnp.full((1,), -1, I32), v[:-1]])
    flags = (blk != prev(blk)).astype(I32) + 2 * (e != prev(e)).astype(I32)
    lanes = lambda v: jnp.pad(v.astype(F32), ((0, 0), (0, LANES - N_EXPERTS)))
    return dict(item_expert=e, item_block=blk, item_lo=lo, item_hi=hi, item_flags=flags,
                n_items=n_items.reshape(1), gseg=gseg.reshape(-1), lstart=lstart.reshape(-1),
                ngran=ngran.reshape(-1), nrows=jnp.sum(lpad, axis=1), total=end[-1:], n_xs=n_xs,
                lstart_v=jnp.broadcast_to(lanes(lstart)[:, None, :], (nt, SUBLANES, LANES)),
                gseg_v=jnp.broadcast_to(lanes(gseg)[:, None, :], (nt, SUBLANES, LANES)))


def _tail_kernel(slot_ref, slotn_ref, x1_ref, ys_ref, gate_ref, p_ref, gple_ref, wg_ref, wp_ref, gfin_ref, o_ref,
                 ybuf0, ybuf1, sem):
    i = pl.program_id(0)
    last = pl.num_programs(0) - 1
    tm = x1_ref.shape[0]

    def row_copy(rows_ref, buf, b, k, r):
        return pltpu.make_async_copy(ys_ref.at[pl.ds(rows_ref[0, 0, k * tm + r], 1)], buf.at[k, pl.ds(r, 1)],
                                     sem.at[b])

    def wait_rows(buf, b):
        for k in range(TOP_K):
            pltpu.make_async_copy(ys_ref.at[pl.ds(0, tm)], buf.at[k], sem.at[b]).wait()

    @pl.when(i == 0)
    def _():
        for k in range(TOP_K):
            lax.fori_loop(0, tm, lambda r, c, k=k: (row_copy(slot_ref, ybuf0, 0, k, r).start(), c)[1], 0, unroll=8)

    def step(cur, nxt, bc, bn):
        wait_rows(cur, bc)
        n = 0
        for k in range(TOP_K):
            for r in range(tm):
                row_copy(slotn_ref, nxt, bn, k, r).start(priority=n % 2)
                n += 1
        gates = gate_ref[...]
        x2 = x1_ref[...]
        for k in range(TOP_K):
            x2 = x2 + gates[:, k:k + 1] * cur[k]
        g = jax.nn.sigmoid(_dot(_rms(x2, gple_ref[...]).astype(BF16), wg_ref[...]))
        x3 = x2 + _dot(p_ref[...].astype(BF16), wp_ref[...]) * g
        o_ref[...] = _rms(x3, gfin_ref[...])

        @pl.when(i == last)
        def _():
            wait_rows(nxt, bn)

    @pl.when(i % 2 == 0)
    def _():
        step(ybuf0, ybuf1, 0, 1)

    @pl.when(i % 2 == 1)
    def _():
        step(ybuf1, ybuf0, 1, 0)


def _tail(slots, x1, ys, gates, p, g_ple, w_gate, w_proj, g_final, *, tm):
    t, d = x1.shape
    nt = t // tm
    rowsp = lambda w: pl.BlockSpec((tm, w), lambda i: (i, 0))
    full = lambda a: pl.BlockSpec(a.shape, lambda i: (0, 0))
    smem = lambda f: pl.BlockSpec((1, 1, TOP_K * tm), f, memory_space=pltpu.SMEM)
    return pl.pallas_call(
        _tail_kernel, grid=(nt,),
        in_specs=[smem(lambda i: (i, 0, 0)), smem(lambda i: (jnp.minimum(i + 1, nt - 1), 0, 0)), rowsp(d),
                  pl.BlockSpec(memory_space=pl.ANY), rowsp(LANES), rowsp(p.shape[1])]
        + [full(a) for a in (g_ple, w_gate, w_proj, g_final)],
        out_specs=rowsp(d), out_shape=jax.ShapeDtypeStruct((t, d), F32),
        scratch_shapes=[pltpu.VMEM((TOP_K, tm, d), F32), pltpu.VMEM((TOP_K, tm, d), F32),
                        pltpu.SemaphoreType.DMA((2,))],
        compiler_params=_params(("arbitrary",)), name="tail",
    )(slots, slots, x1, ys, gates, p, g_ple, w_gate, w_proj, g_final)


def _block_diag(a, n):
    j, _, r, c = a.shape
    eye = jnp.eye(n, dtype=a.dtype)
    return jnp.einsum("ab,jarc->jarbc", eye, a).reshape(j, n * r, n * c)


def _mixer_tail(groups, lw, *, bm):
    cnt = jnp.zeros((SUBLANES, LANES), F32)
    routed = []
    for x_tok, attn, ssm_n, p_tok, tm in groups:
        tm = min(tm, x_tok.shape[0])
        x1, hm, idx, gates, rank, before, cnt = _outproj(
            x_tok, attn, ssm_n, lw["g_attn"], lw["wo_a"], lw["wo_s"], lw["g_ffn"], lw["wr_pad"], lw["br_pad"], cnt,
            tm=tm)
        routed.append((x1, hm, idx, gates, rank, before, p_tok, tm))
    before_all = jnp.concatenate([r[5][:, 0, :N_EXPERTS] for r in routed], axis=0)
    route = _route(cnt[0, :N_EXPERTS], before_all, t=sum(r[0].shape[0] for r in routed), bm=bm)
    xs, tile0, slots = None, 0, []
    for n, (x1, hm, idx, gates, rank, before, p_tok, tm) in enumerate(routed):
        nt = x1.shape[0] // tm
        xs, slot = _dispatch(route, tile0, hm, idx, rank, before, xs, tm=tm, fill_tail=n == len(routed) - 1)
        slots.append(jnp.transpose(slot[:, :TOP_K].reshape(nt, tm, TOP_K), (0, 2, 1)).reshape(nt, 1, TOP_K * tm))
        tile0 += nt
    ys = _experts(route, xs, lw["w_up"], lw["b_up"], lw["w_down"], lw["b_down"], bm=bm)
    return [_tail(s, x1, ys, gates, p_tok, lw["g_ple"], lw["w_ple_gate"], lw["w_ple_proj"], lw["g_final"], tm=tm)
            for s, (x1, hm, idx, gates, rank, before, p_tok, tm) in zip(slots, routed)]


def kernel(x_prompt, x_sample, p_prompt, p_sample, cache_k, cache_v, cache_logf, state_ssm_re, state_ssm_im,
           page_table, g_mix, w_in, b_fgate, lam_re, lam_im, log_step, b_ssm_re, b_ssm_im, c_ssm_re, c_ssm_im,
           d_ssm, w_glu, b_glu, g_attn_out, g_ssm_out, w_out, g_ffn, w_router, b_router, w_up, b_up, w_down,
           b_down, g_ple, w_ple_gate, w_ple_proj, g_final):
    depth = w_in.shape[0]
    assert depth == 1
    bp, sp, d = x_prompt.shape
    bd, sd, _ = x_sample.shape
    n_heads = b_fgate.shape[1]
    aw = n_heads * HEAD_DIM
    n_groups, n_state = lam_re.shape[1:]
    sw = n_groups * SSM_GROUP
    half = n_groups * n_state
    assert sw == SSM_LANE_BLOCKS * LANES and w_in.shape[2] == 3 * aw + n_heads + sw

    wi = w_in[0]
    wq, wk, wv = wi[:, :aw], wi[:, aw:2 * aw], wi[:, 2 * aw:3 * aw]
    wf, wu = wi[:, 3 * aw:3 * aw + n_heads], wi[:, 3 * aw + n_heads:]
    w_all = jnp.concatenate([wq, wk, wv, wu, jnp.pad(wf, ((0, 0), (0, LANES - n_heads)))], axis=1).astype(BF16)
    wft = jnp.pad(wf.T, ((0, 2 * SUBLANES - n_heads), (0, 0))).astype(BF16)
    bf_col = jnp.pad(b_fgate[0][:, None], ((0, 2 * SUBLANES - n_heads), (0, 0)))
    bf_row = jnp.pad(b_fgate, ((0, 0), (0, LANES - n_heads)))

    abar_re, abar_im, bbr_t, bbi_t = _s5_params(
        lam_re[0], lam_im[0], log_step[0], jnp.transpose(b_ssm_re[0], (0, 2, 1)), jnp.transpose(b_ssm_im[0], (0, 2, 1)))
    gpb = n_groups // SSM_LANE_BLOCKS
    blk4 = lambda a: a.reshape((SSM_LANE_BLOCKS, gpb) + a.shape[1:])
    bd_mat = jnp.concatenate([_block_diag(blk4(bbr_t), gpb), _block_diag(blk4(bbi_t), gpb)], axis=-1).astype(BF16)
    cdr = _block_diag(blk4(jnp.transpose(c_ssm_re[0], (0, 2, 1))), gpb).astype(BF16)
    cdi = _block_diag(blk4(jnp.transpose(c_ssm_im[0], (0, 2, 1))), gpb).astype(BF16)
    ssm_w = (abar_re.reshape(1, half), abar_im.reshape(1, half), bd_mat, cdr, cdi, d_ssm[0].reshape(1, sw),
             w_glu[0].astype(BF16), b_glu, g_ssm_out)

    lw = dict(g_attn=g_attn_out, wo_a=w_out[0, :aw].astype(BF16), wo_s=w_out[0, aw:].astype(BF16), g_ffn=g_ffn,
              wr_pad=jnp.pad(w_router[0], ((0, 0), (0, LANES - N_EXPERTS))).astype(BF16),
              br_pad=jnp.pad(b_router, ((0, 0), (0, LANES - N_EXPERTS))),
              w_up=w_up[0], b_up=b_up[0][:, None, :], w_down=w_down[0],
              b_down=b_down[0][:, None, :], g_ple=g_ple, w_ple_gate=w_ple_gate[0].astype(BF16),
              w_ple_proj=w_ple_proj[0].astype(BF16), g_final=g_final.reshape(1, d))

    q, k, v, kb, vb, u, lft, ct = _inproj_prompt(x_prompt, g_mix, w_all, wft, bf_col, aw=aw, n_heads=n_heads, tm=512)
    attn = _fox_prompt(q, kb, vb, ct, tq=512)
    n_time = min(64, sp)
    zeros_h = jnp.zeros((bp, half), F32)
    ssm_tb, hr_p, hi_p = _ssm(jnp.transpose(u, (1, 0, 2)).reshape(sp * bp, sw), zeros_h, zeros_h, *ssm_w,
                              n_batch=bp, n_time=n_time)
    ssm_p = jnp.transpose(ssm_tb.reshape(sp, bp, sw), (1, 0, 2)).reshape(bp * sp, sw)

    ts = bd * sd
    qs, ks, vs, us, lfs = _inproj_sample(x_sample.reshape(ts, d), g_mix, w_all, bf_row, aw=aw)
    lfs = lfs[:, :n_heads]
    kt_cache = jnp.transpose(cache_k[0], (0, 2, 3, 1))
    vt_cache = jnp.transpose(cache_v[0], (0, 2, 3, 1))
    lft_cache = jnp.transpose(cache_logf[0], (0, 2, 1))
    attn_s = _fox_decode(page_table, qs.reshape(bd, sd, aw), ks.reshape(bd, sd, aw), vs.reshape(bd, sd, aw),
                         jnp.transpose(lfs.reshape(bd, sd, n_heads), (0, 2, 1)), kt_cache, vt_cache, lft_cache,
                         n_group=32)
    ssm_s_tb, hr_s, hi_s = _ssm(jnp.transpose(us.reshape(bd, sd, sw), (1, 0, 2)).reshape(ts, sw),
                                state_ssm_re[0].reshape(bd, half), state_ssm_im[0].reshape(bd, half), *ssm_w,
                                n_batch=bd, n_time=sd)
    ssm_s = jnp.transpose(ssm_s_tb.reshape(sd, bd, sw), (1, 0, 2)).reshape(ts, sw)

    (y_p,) = _mixer_tail(
        [(x_prompt.reshape(bp * sp, d), attn.reshape(bp * sp, aw), ssm_p, p_prompt[0].reshape(bp * sp, -1), 512)],
        lw, bm=512)
    (y_s,) = _mixer_tail(
        [(x_sample.reshape(ts, d), attn_s.reshape(ts, aw), ssm_s, p_sample[0].reshape(ts, -1), 128)], lw, bm=128)

    heads = lambda a, b_, s_: a.reshape(1, b_, s_, n_heads, HEAD_DIM)
    state = lambda a, b_: a.reshape(1, b_, n_groups, n_state)
    return (y_p.reshape(bp, sp, d), y_s.reshape(bd, sd, d),
            heads(k, bp, sp), heads(v, bp, sp), jnp.transpose(lft, (0, 2, 1))[None],
            state(hr_p, bp), state(hi_p, bp),
            heads(ks, bd, sd), heads(vs, bd, sd), lfs.reshape(1, bd, sd, n_heads),
            state(hr_s, bd), state(hi_s, bd))
```

```python
import functools

import jax
import jax.numpy as jnp
from jax import lax
from jax.experimental import pallas as pl
from jax.experimental.pallas import tpu as pltpu

F32 = jnp.float32
BF16 = jnp.bfloat16
I32 = jnp.int32

RMS_EPS = 1e-6
LOG2E = 1.4426950408889634
HEAD_DIM = 64
HEADS_PER_LANE_TILE = 2
LANES = 128
SUBLANES = 8
SSM_GROUP = 16
SSM_STATE = 64
SSM_LANE_BLOCKS = 4
N_EXPERTS = 32
TOP_K = 4
GRANULE = 8
SWIGLU_ALPHA = 1.702
SWIGLU_LIMIT = 7.0
VMEM_LIMIT = 56 * 1024 * 1024


def _params(sem, vmem=VMEM_LIMIT):
    return pltpu.CompilerParams(dimension_semantics=sem, vmem_limit_bytes=vmem)


def _rms(x, g):
    return x * lax.rsqrt(jnp.mean(x * x, axis=-1, keepdims=True) + RMS_EPS) * g


def _log_sigmoid(x):
    return jnp.minimum(x, 0.0) - jnp.log1p(jnp.exp(-jnp.abs(x)))


def _split3(x):
    hi = x.astype(BF16)
    r1 = x - hi.astype(F32)
    mid = r1.astype(BF16)
    lo = (r1 - mid.astype(F32)).astype(BF16)
    return hi, mid, lo


def _dot(a, b):
    return jnp.dot(a, b, preferred_element_type=F32)


def _dot_nt(a, b):
    return lax.dot_general(a, b, (((1,), (1,)), ((), ())), preferred_element_type=F32)


def _dot3(x, m):
    hi, mid, lo = _split3(x)
    return _dot(hi, m) + _dot(mid, m) + _dot(lo, m)


def _s5_params_kernel(lr_ref, li_ref, ls_ref, brt_ref, bit_ref, ar_ref, ai_ref, bbr_ref, bbi_ref):
    lr, li = lr_ref[...], li_ref[...]
    step = jnp.exp(ls_ref[...])
    mag = jnp.exp(lr * step)
    a_re, a_im = mag * jnp.cos(li * step), mag * jnp.sin(li * step)
    nr, ni = a_re - 1.0, a_im
    den = lr * lr + li * li
    coef_re = (nr * lr + ni * li) / den
    coef_im = (ni * lr - nr * li) / den
    ar_ref[...] = a_re
    ai_ref[...] = a_im
    br, bi = brt_ref[...], bit_ref[...]
    bbr_ref[...] = coef_re * br - coef_im * bi
    bbi_ref[...] = coef_re * bi + coef_im * br


def _s5_params(lam_re, lam_im, log_step, b_re_t, b_im_t):
    g, p = lam_re.shape
    c = b_re_t.shape[1]
    return pl.pallas_call(
        _s5_params_kernel,
        out_shape=(jax.ShapeDtypeStruct((g, 1, p), F32), jax.ShapeDtypeStruct((g, 1, p), F32),
                   jax.ShapeDtypeStruct((g, c, p), F32), jax.ShapeDtypeStruct((g, c, p), F32)),
        name="s5_params",
    )(lam_re.reshape(g, 1, p), lam_im.reshape(g, 1, p), log_step.reshape(g, 1, 1), b_re_t, b_im_t)


def _inproj_core(x, g_ref, w_ref, aw):
    h = _rms(x, g_ref[...]).astype(BF16)
    z = _dot(h, w_ref[...])
    return h, z


def _inproj_prompt_kernel(x_ref, g_ref, w_ref, wft_ref, bf_ref, utri_ref,
                          q_ref, k_ref, v_ref, kb_ref, vb_ref, u_ref, lft_ref, ct_ref, carry_ref, *, aw, scale):
    tm = x_ref.shape[1]
    h, z = _inproj_core(x_ref[0], g_ref, w_ref, aw)
    q_ref[0] = (z[:, :aw] * scale).astype(BF16)
    k = z[:, aw:2 * aw]
    v = z[:, 2 * aw:3 * aw]
    k_ref[0] = k
    v_ref[0] = v
    kb_ref[0] = k.astype(BF16)
    vb_ref[0] = v.astype(BF16)
    u_ref[0] = z[:, 3 * aw:4 * aw]
    ft = _dot_nt(wft_ref[...], h) + bf_ref[...]
    lft = _log_sigmoid(ft)
    n_heads = lft_ref.shape[1]
    lft_ref[0] = lft[:n_heads]

    @pl.when(pl.program_id(1) == 0)
    def _():
        carry_ref[...] = jnp.zeros_like(carry_ref)

    cs = _dot3(lft, utri_ref[...]) + carry_ref[:, :1]
    carry_ref[...] = jnp.broadcast_to(cs[:, tm - 1:tm], carry_ref.shape)
    ct_ref[0] = cs[:n_heads]


def _inproj_prompt(x, g_mix, w_all, wft, bf_col, *, aw, n_heads, tm):
    b, s, d = x.shape
    tm = min(tm, s)
    assert s % tm == 0
    ii = lax.broadcasted_iota(I32, (tm, tm), 0)
    jj = lax.broadcasted_iota(I32, (tm, tm), 1)
    utri = (ii <= jj).astype(BF16)
    row = lambda bi, si: (bi, si, 0)
    col = lambda bi, si: (bi, 0, si)
    const = lambda bi, si: (0, 0)
    wide = pl.BlockSpec((1, tm, aw), row)
    tall = pl.BlockSpec((1, n_heads, tm), col)
    return pl.pallas_call(
        functools.partial(_inproj_prompt_kernel, aw=aw, scale=HEAD_DIM ** -0.5 * LOG2E),
        grid=(b, s // tm),
        in_specs=[pl.BlockSpec((1, tm, d), row), pl.BlockSpec((1, d), const), pl.BlockSpec(w_all.shape, const),
                  pl.BlockSpec(wft.shape, const), pl.BlockSpec(bf_col.shape, const), pl.BlockSpec((tm, tm), const)],
        out_specs=[wide, wide, wide, wide, wide, wide, tall, tall],
        out_shape=[jax.ShapeDtypeStruct((b, s, aw), BF16), jax.ShapeDtypeStruct((b, s, aw), F32),
                   jax.ShapeDtypeStruct((b, s, aw), F32), jax.ShapeDtypeStruct((b, s, aw), BF16),
                   jax.ShapeDtypeStruct((b, s, aw), BF16), jax.ShapeDtypeStruct((b, s, aw), F32),
                   jax.ShapeDtypeStruct((b, n_heads, s), F32), jax.ShapeDtypeStruct((b, n_heads, s), F32)],
        scratch_shapes=[pltpu.VMEM((2 * SUBLANES, LANES), F32)],
        compiler_params=_params(("arbitrary", "arbitrary")),
        name="inproj_prompt",
    )(x, g_mix, w_all, wft, bf_col, utri)


def _inproj_sample_kernel(x_ref, g_ref, w_ref, bf_ref, q_ref, k_ref, v_ref, u_ref, lf_ref, *, aw, scale):
    _, z = _inproj_core(x_ref[...], g_ref, w_ref, aw)
    q_ref[...] = (z[:, :aw] * scale).astype(BF16)
    k_ref[...] = z[:, aw:2 * aw]
    v_ref[...] = z[:, 2 * aw:3 * aw]
    u_ref[...] = z[:, 3 * aw:4 * aw]
    lf_ref[...] = _log_sigmoid(z[:, 4 * aw:] + bf_ref[...])


def _inproj_sample(x, g_mix, w_all, bf_row, *, aw):
    t, d = x.shape
    outs = [jax.ShapeDtypeStruct((t, aw), BF16)] + [jax.ShapeDtypeStruct((t, aw), F32)] * 3
    outs.append(jax.ShapeDtypeStruct((t, LANES), F32))
    return pl.pallas_call(
        functools.partial(_inproj_sample_kernel, aw=aw, scale=HEAD_DIM ** -0.5),
        out_shape=outs, compiler_params=_params(None), name="inproj_sample",
    )(x, g_mix, w_all, bf_row)


_L_ONE, _L_BIAS = HEAD_DIM, HEAD_DIM + 3


def _bias_rows(c_row, piece_row0, ones_row0, n_ones, width):
    hi, mid, lo = (p.astype(F32) for p in _split3(c_row))
    r = lax.broadcasted_iota(I32, (LANES, width), 0)
    out = jnp.where((r >= ones_row0) & (r < ones_row0 + n_ones), 1.0, 0.0)
    for i, piece in enumerate((hi, mid, lo)):
        out = jnp.where(r == piece_row0 + i, piece, out)
    return out


def _fox_prompt_kernel(q_ref, k_ref, v_ref, ct_ref, o_ref, kaug, vaug, *, tq):
    qi = pl.program_id(2)
    pair = pl.program_id(1)
    n_kv = k_ref.shape[1] // tq
    lane = lax.broadcasted_iota(I32, (tq, LANES), 1)
    row = lax.broadcasted_iota(I32, (tq, tq), 0)
    colm = lax.broadcasted_iota(I32, (tq, tq), 1)
    src = lax.broadcasted_iota(I32, (LANES, LANES), 0)
    dst = lax.broadcasted_iota(I32, (LANES, LANES), 1)
    sels = [((src == dst + hh * HEAD_DIM) & (dst < HEAD_DIM)).astype(BF16) for hh in range(HEADS_PER_LANE_TILE)]

    @pl.when(qi == 0)
    def _():
        for hh in range(HEADS_PER_LANE_TILE):
            head = pair * HEADS_PER_LANE_TILE + hh
            for c in range(n_kv):
                rows = slice(c * tq, (c + 1) * tq)
                ck = ct_ref[0, head, c:c + 1, :] * LOG2E
                kb = _bias_rows(-ck, _L_ONE, _L_BIAS, 3, tq).T
                kaug[hh, rows, :] = (_dot(k_ref[0, rows, :], sels[hh]) + kb).astype(BF16)
                vaug[hh, rows, :] = (_dot(v_ref[0, rows, :], sels[hh])
                                     + jnp.where(lane == _L_ONE, 1.0, 0.0)).astype(BF16)

    qas = []
    for hh in range(HEADS_PER_LANE_TILE):
        cq = ct_ref[0, pair * HEADS_PER_LANE_TILE + hh, pl.ds(qi, 1), :] * LOG2E
        qas.append((_dot(q_ref[0], sels[hh]) + _bias_rows(cq, _L_BIAS, _L_ONE, 3, tq).T).astype(BF16))

    def tile(j, carry, masked):
        start = pl.multiple_of(j * tq, tq)
        new = []
        for hh in range(HEADS_PER_LANE_TILE):
            m, acc = carry[hh]
            ka = kaug[hh, pl.ds(start, tq), :]
            va = vaug[hh, pl.ds(start, tq), :]
            s = _dot_nt(qas[hh], ka)
            if masked:
                s = jnp.where(colm <= row, s, -jnp.inf)
            m_new = jnp.maximum(m, jnp.max(s, axis=-1, keepdims=True))
            p = jnp.exp2(s - m_new).astype(BF16)
            acc = jnp.exp2(m - m_new) * acc + _dot(p, va)
            new.append((m_new, acc))
        return tuple(new)

    init = tuple((jnp.full((tq, 1), -jnp.inf, F32), jnp.zeros((tq, LANES), F32))
                 for _ in range(HEADS_PER_LANE_TILE))
    carry = lax.fori_loop(0, qi // 2, lambda jj, c: tile(2 * jj + 1, tile(2 * jj, c, False), False), init)
    carry = lax.cond(qi % 2 == 1, lambda c: tile(qi - 1, c, False), lambda c: c, carry)
    (_, acc0), (_, acc1) = tile(qi, carry, True)
    out0 = acc0 / acc0[:, _L_ONE:_L_ONE + 1]
    out1 = acc1 / acc1[:, _L_ONE:_L_ONE + 1]
    o_ref[0] = jnp.where(lane < HEAD_DIM, out0, pltpu.roll(out1, HEAD_DIM, 1))


def _fox_prompt(q, kb, vb, ct, *, tq):
    b, s, aw = q.shape
    tq = min(tq, s)
    assert s % tq == 0
    n_pairs = aw // LANES
    ct = ct.reshape(b, ct.shape[1], s // tq, tq)
    return pl.pallas_call(
        functools.partial(_fox_prompt_kernel, tq=tq),
        grid=(b, n_pairs, s // tq),
        in_specs=[pl.BlockSpec((1, tq, LANES), lambda bi, p, qi: (bi, qi, p)),
                  pl.BlockSpec((1, s, LANES), lambda bi, p, qi: (bi, 0, p)),
                  pl.BlockSpec((1, s, LANES), lambda bi, p, qi: (bi, 0, p)),
                  pl.BlockSpec((1,) + ct.shape[1:], lambda bi, p, qi: (bi, 0, 0, 0))],
        out_specs=pl.BlockSpec((1, tq, LANES), lambda bi, p, qi: (bi, qi, p)),
        out_shape=jax.ShapeDtypeStruct((b, s, aw), F32),
        scratch_shapes=[pltpu.VMEM((HEADS_PER_LANE_TILE, s, LANES), BF16),
                        pltpu.VMEM((HEADS_PER_LANE_TILE, s, LANES), BF16)],
        compiler_params=_params(("arbitrary", "arbitrary", "arbitrary")),
        name="fox_prompt",
    )(q, kb, vb, ct)


def _fox_decode_kernel(pt_ref, q_ref, kn_ref, vn_ref, lfn_ref, ustrict_ref, *rest, n_group, n_new):
    kt_refs = rest[:n_group]
    vt_refs = rest[n_group:2 * n_group]
    lf_refs = rest[2 * n_group:3 * n_group]
    o_ref, m_ref, l_ref, acc_ref, tail_ref = rest[3 * n_group:]
    j = pl.program_id(1)
    n_heads = lfn_ref.shape[1]
    width = q_ref.shape[2]
    rows = n_new * n_heads

    lane_head = lax.broadcasted_iota(I32, (n_heads, width), 1) // HEAD_DIM
    sub = lax.broadcasted_iota(I32, (n_heads, width), 0)
    hmask = lane_head == sub
    q = q_ref[0].astype(F32)
    qexp = jnp.concatenate([jnp.where(hmask, jnp.broadcast_to(q[t:t + 1], (n_heads, width)), 0.0)
                            for t in range(n_new)], axis=0)
    trow = lax.broadcasted_iota(I32, (rows, 1), 0) // n_heads

    @pl.when(j == 0)
    def _():
        lfn = lfn_ref[0]
        kn, vn = kn_ref[0], vn_ref[0]
        run = jnp.zeros((n_heads, 1), F32)
        logits = []
        for sp in range(n_new):
            run = run + lfn[:, sp:sp + 1]
            bias = jnp.concatenate([run] * n_new, axis=0)
            sc = jnp.sum(qexp * kn[sp:sp + 1], axis=-1, keepdims=True) - bias
            logits.append(jnp.where(trow >= sp, sc, -jnp.inf))
        m = logits[0]
        for sc in logits[1:]:
            m = jnp.maximum(m, sc)
        l = jnp.zeros((rows, 1), F32)
        acc = jnp.zeros((rows, width), F32)
        for sp in range(n_new):
            p = jnp.exp(logits[sp] - m)
            l = l + p
            acc = acc + p * vn[sp:sp + 1]
        m_ref[...] = m
        l_ref[...] = l
        acc_ref[...] = acc
        tail_ref[...] = jnp.zeros_like(tail_ref)

    qb = qexp.astype(BF16)
    tail = tail_ref[:, :1]
    scores = []
    lfs = [lf_refs[g][0] for g in range(n_group)]
    if n_group % 2:
        lfs.append(jnp.zeros_like(lfs[0]))
    within = _dot3(jnp.concatenate(lfs, axis=0), ustrict_ref[...])
    for g in range(n_group):
        kt = kt_refs[g][0].reshape(width, -1).astype(BF16)
        suffix = within[g * n_heads:(g + 1) * n_heads] + tail
        tail = tail + jnp.sum(lfs[g], axis=-1, keepdims=True)
        scores.append(_dot(qb, kt) + jnp.concatenate([suffix] * n_new, axis=0))
    tail_ref[...] = jnp.broadcast_to(tail, tail_ref.shape)
    m_old = m_ref[...]
    m_new = m_old
    for s in scores:
        m_new = jnp.maximum(m_new, jnp.max(s, axis=-1, keepdims=True))
    alpha = jnp.exp(m_old - m_new)
    l = alpha * l_ref[...]
    acc = alpha * acc_ref[...]
    for g, s in enumerate(scores):
        p = jnp.exp(s - m_new)
        l = l + jnp.sum(p, axis=-1, keepdims=True)
        acc = acc + _dot_nt(p.astype(BF16), vt_refs[g][0].reshape(width, -1).astype(BF16))
    l_ref[...] = l
    acc_ref[...] = acc
    m_ref[...] = m_new

    @pl.when(j == pl.num_programs(1) - 1)
    def _():
        o = acc_ref[...] / l_ref[...]
        outs = [jnp.sum(jnp.where(hmask, o[t * n_heads:(t + 1) * n_heads], 0.0), axis=0, keepdims=True)
                for t in range(n_new)]
        o_ref[0] = jnp.concatenate(outs, axis=0)


def _fox_decode(page_table, q, k_new, v_new, lfn_t, kt_cache, vt_cache, lft_cache, *, n_group):
    bd, n_new, width = q.shape
    n_pages = page_table.shape[1]
    n_heads, page = lft_cache.shape[1:]
    n_group = min(n_group, n_pages)
    assert n_pages % n_group == 0
    n_steps = n_pages // n_group
    ii = lax.broadcasted_iota(I32, (page, page), 0)
    jj = lax.broadcasted_iota(I32, (page, page), 1)
    ustrict = (ii > jj).astype(BF16)

    def page_map(g, ndim):
        def index_map(b, j, pt):
            return (pt[b, n_pages - 1 - (j * n_group + g)],) + (0,) * ndim
        return index_map

    per_batch = lambda b, j, pt: (b, 0, 0)
    in_specs = [pl.BlockSpec((1, n_new, width), per_batch), pl.BlockSpec((1, n_new, width), per_batch),
                pl.BlockSpec((1, n_new, width), per_batch), pl.BlockSpec((1, n_heads, n_new), per_batch),
                pl.BlockSpec((page, page), lambda b, j, pt: (0, 0))]
    in_specs += [pl.BlockSpec((1,) + kt_cache.shape[1:], page_map(g, 3)) for g in range(n_group)]
    in_specs += [pl.BlockSpec((1,) + vt_cache.shape[1:], page_map(g, 3)) for g in range(n_group)]
    in_specs += [pl.BlockSpec((1, n_heads, page), page_map(g, 2)) for g in range(n_group)]
    rows = n_new * n_heads
    grid_spec = pltpu.PrefetchScalarGridSpec(
        num_scalar_prefetch=1, grid=(bd, n_steps), in_specs=in_specs,
        out_specs=pl.BlockSpec((1, n_new, width), per_batch),
        scratch_shapes=[pltpu.VMEM((rows, 1), F32), pltpu.VMEM((rows, 1), F32), pltpu.VMEM((rows, width), F32),
                        pltpu.VMEM((n_heads, LANES), F32)])
    return pl.pallas_call(
        functools.partial(_fox_decode_kernel, n_group=n_group, n_new=n_new),
        grid_spec=grid_spec, out_shape=jax.ShapeDtypeStruct((bd, n_new, width), F32),
        compiler_params=_params(("arbitrary", "arbitrary")), name="fox_decode",
    )(page_table, q, k_new, v_new, lfn_t, ustrict,
      *([kt_cache] * n_group), *([vt_cache] * n_group), *([lft_cache] * n_group))


def _ssm_kernel(u_ref, h0r_ref, h0i_ref, ar_ref, ai_ref, bd_ref, cdr_ref, cdi_ref, d_ref, wglu_ref, bglu_ref,
                gout_ref, o_ref, hr_ref, hi_ref, bu_ref, h_ref, *, n_batch, n_time):
    i = pl.program_id(0)
    half = ar_ref.shape[1]
    blk = half // SSM_LANE_BLOCKS
    rows = n_batch * n_time

    @pl.when(i == 0)
    def _():
        h_ref[:, :half] = h0r_ref[...]
        h_ref[:, half:] = h0i_ref[...]

    u = u_ref[...]
    ub = u.astype(BF16)
    ys = []
    for jb in range(SSM_LANE_BLOCKS):
        lo_r, lo_i = jb * blk, half + jb * blk
        bu = _dot(ub[:, jb * LANES:(jb + 1) * LANES], bd_ref[jb])
        bu_ref[:, lo_r:lo_r + blk] = bu[:, :blk]
        bu_ref[:, lo_i:lo_i + blk] = bu[:, blk:]
        a_r = jnp.broadcast_to(ar_ref[:, lo_r:lo_r + blk], (SUBLANES, blk))
        a_i = jnp.broadcast_to(ai_ref[:, lo_r:lo_r + blk], (SUBLANES, blk))
        for bg in range(n_batch // SUBLANES):
            b0 = bg * SUBLANES
            h_r = h_ref[b0:b0 + SUBLANES, lo_r:lo_r + blk]
            h_i = h_ref[b0:b0 + SUBLANES, lo_i:lo_i + blk]
            for t in range(n_time):
                r0 = t * n_batch + b0
                n_r = a_r * h_r - a_i * h_i + bu_ref[r0:r0 + SUBLANES, lo_r:lo_r + blk]
                n_i = a_r * h_i + a_i * h_r + bu_ref[r0:r0 + SUBLANES, lo_i:lo_i + blk]
                bu_ref[r0:r0 + SUBLANES, lo_r:lo_r + blk] = n_r
                bu_ref[r0:r0 + SUBLANES, lo_i:lo_i + blk] = n_i
                h_r, h_i = n_r, n_i
            h_ref[b0:b0 + SUBLANES, lo_r:lo_r + blk] = h_r
            h_ref[b0:b0 + SUBLANES, lo_i:lo_i + blk] = h_i
        hre = bu_ref[:, lo_r:lo_r + blk].astype(BF16)
        him = bu_ref[:, lo_i:lo_i + blk].astype(BF16)
        ys.append(_dot(hre, cdr_ref[jb]) - _dot(him, cdi_ref[jb]))
    y = jnp.concatenate(ys, axis=-1) + d_ref[...] * u
    y = jax.nn.gelu(y, approximate=True)
    y = y * jax.nn.sigmoid(_dot(y.astype(BF16), wglu_ref[...]) + bglu_ref[...])
    o_ref[...] = _rms(y, gout_ref[...]).astype(BF16)

    @pl.when(i == pl.num_programs(0) - 1)
    def _():
        hr_ref[...] = h_ref[:, :half]
        hi_ref[...] = h_ref[:, half:]


def _ssm(u_tb, h0r, h0i, abar_re, abar_im, bd, cdr, cdi, d_row, w_glu, b_glu, g_out, *, n_batch, n_time):
    total, width = u_tb.shape
    half = abar_re.shape[1]
    rows = n_batch * n_time
    assert total % rows == 0 and n_batch % SUBLANES == 0
    const2 = lambda i: (0, 0)
    const3 = lambda i: (0, 0, 0)
    full = lambda a: pl.BlockSpec(a.shape, const2 if a.ndim == 2 else const3)
    return pl.pallas_call(
        functools.partial(_ssm_kernel, n_batch=n_batch, n_time=n_time),
        grid=(total // rows,),
        in_specs=[pl.BlockSpec((rows, width), lambda i: (i, 0))] + [full(a) for a in (
            h0r, h0i, abar_re, abar_im, bd, cdr, cdi, d_row, w_glu, b_glu, g_out)],
        out_specs=[pl.BlockSpec((rows, width), lambda i: (i, 0)), pl.BlockSpec((n_batch, half), const2),
                   pl.BlockSpec((n_batch, half), const2)],
        out_shape=[jax.ShapeDtypeStruct((total, width), BF16), jax.ShapeDtypeStruct((n_batch, half), F32),
                   jax.ShapeDtypeStruct((n_batch, half), F32)],
        scratch_shapes=[pltpu.VMEM((rows, 2 * half), F32), pltpu.VMEM((n_batch, 2 * half), F32)],
        compiler_params=_params(("arbitrary",)), name="ssm",
    )(u_tb, h0r, h0i, abar_re, abar_im, bd, cdr, cdi, d_row, w_glu, b_glu, g_out)


def _outproj_kernel(x_ref, a_ref, s_ref, ga_ref, woa_ref, wos_ref, gffn_ref, wr_ref, br_ref, lstrict_ref, cnt0_ref,
                    x1_ref, hm_ref, idx_ref, gate_ref, rank_ref, before_ref, cnt_ref, cnt_sc):
    i = pl.program_id(0)
    tm = x_ref.shape[0]

    @pl.when(i == 0)
    def _():
        cnt_sc[...] = cnt0_ref[...]

    before_ref[0] = cnt_sc[...]

    an = _rms(a_ref[...], ga_ref[...]).astype(BF16)
    x1 = x_ref[...] + _dot(an, woa_ref[...]) + _dot(s_ref[...], wos_ref[...])
    x1_ref[...] = x1
    hm = _rms(x1, gffn_ref[...])
    hm_ref[...] = hm
    logits = _dot(hm.astype(BF16), wr_ref[...]) + br_ref[...]
    lane_i = lax.broadcasted_iota(I32, (tm, LANES), 1)
    lane = lane_i.astype(F32)
    work = jnp.where(lane_i < N_EXPERTS, logits, -jnp.inf)
    vals, idxs = [], []
    for _ in range(TOP_K):
        mx = jnp.max(work, axis=-1, keepdims=True)
        ix = jnp.min(jnp.where(work == mx, lane, float(LANES)), axis=-1, keepdims=True)
        vals.append(mx)
        idxs.append(ix)
        work = jnp.where(lane == ix, -jnp.inf, work)
    exps = [jnp.exp(v - vals[0]) for v in vals]
    denom = exps[0]
    for e in exps[1:]:
        denom = denom + e
    onehot = jnp.zeros((tm, LANES), F32)
    for ix in idxs:
        onehot = onehot + (lane == ix).astype(F32)
    before = _dot(lstrict_ref[...], onehot.astype(BF16)) + cnt_sc[:1, :]
    idx_out = jnp.zeros((tm, LANES), I32)
    gate_out = jnp.zeros((tm, LANES), F32)
    rank_out = jnp.zeros((tm, LANES), I32)
    for k in range(TOP_K):
        rk = jnp.sum(jnp.where(lane == idxs[k], before, 0.0), axis=-1, keepdims=True)
        idx_out = jnp.where(lane_i == k, idxs[k].astype(I32), idx_out)
        gate_out = jnp.where(lane_i == k, exps[k] / denom, gate_out)
        rank_out = jnp.where(lane_i == k, rk.astype(I32), rank_out)
    idx_ref[...] = idx_out
    gate_ref[...] = gate_out
    rank_ref[...] = rank_out
    cnt = cnt_sc[:1, :] + jnp.sum(onehot, axis=0, keepdims=True)
    cnt_sc[...] = jnp.broadcast_to(cnt, cnt_sc.shape)
    cnt_ref[...] = jnp.broadcast_to(cnt, cnt_ref.shape)


def _outproj(x, attn, ssm_n, g_attn, wo_a, wo_s, g_ffn, wr_pad, br_pad, cnt0, *, tm):
    t, d = x.shape
    assert t % tm == 0
    ii = lax.broadcasted_iota(I32, (tm, tm), 0)
    jj = lax.broadcasted_iota(I32, (tm, tm), 1)
    lstrict = (jj < ii).astype(BF16)
    rowsp = lambda w: pl.BlockSpec((tm, w), lambda i: (i, 0))
    full = lambda a: pl.BlockSpec(a.shape, lambda i: (0, 0))
    aw = attn.shape[1]
    return pl.pallas_call(
        _outproj_kernel, grid=(t // tm,),
        in_specs=[rowsp(d), rowsp(aw), rowsp(ssm_n.shape[1])] + [full(a) for a in (
            g_attn, wo_a, wo_s, g_ffn, wr_pad, br_pad, lstrict, cnt0)],
        out_specs=[rowsp(d), rowsp(d), rowsp(LANES), rowsp(LANES), rowsp(LANES),
                   pl.BlockSpec((1, SUBLANES, LANES), lambda i: (i, 0, 0)),
                   pl.BlockSpec((SUBLANES, LANES), lambda i: (0, 0))],
        out_shape=[jax.ShapeDtypeStruct((t, d), F32), jax.ShapeDtypeStruct((t, d), F32),
                   jax.ShapeDtypeStruct((t, LANES), I32), jax.ShapeDtypeStruct((t, LANES), F32),
                   jax.ShapeDtypeStruct((t, LANES), I32), jax.ShapeDtypeStruct((t // tm, SUBLANES, LANES), F32),
                   jax.ShapeDtypeStruct((SUBLANES, LANES), F32)],
        scratch_shapes=[pltpu.VMEM((SUBLANES, LANES), F32)],
        compiler_params=_params(("arbitrary",)), name="outproj_router",
    )(x, attn, ssm_n, g_attn, wo_a, wo_s, g_ffn, wr_pad, br_pad, lstrict, cnt0)


def _dispatch_kernel(gseg_ref, lstart_ref, ngran_ref, nrows_ref, total_ref, hm_ref, idx_ref, rank_ref, before_ref,
                     lstartv_ref, gsegv_ref, *rest, chunk, has_prev, fill_tail):
    xs_ref, slot_ref, sorted_ref, zeros_ref, sem = (rest[:1] + rest[2:]) if has_prev else rest
    i = pl.program_id(0)
    tm = hm_ref.shape[0]
    lane = lax.broadcasted_iota(I32, (tm, LANES), 1)
    idx = idx_ref[...]
    rank = rank_ref[...].astype(F32)
    local_off = lstartv_ref[0][:1] - before_ref[0][:1]
    global_off = gsegv_ref[0][:1] - before_ref[0][:1]
    lslots = jnp.full((tm, LANES), -1.0, F32)
    gslots = jnp.zeros((tm, LANES), I32)
    for k in range(TOP_K):
        sel = lane == idx[:, k:k + 1]
        r_k = rank[:, k:k + 1]
        ls = jnp.sum(jnp.where(sel, local_off, 0.0), axis=-1, keepdims=True) + r_k
        gs = jnp.sum(jnp.where(sel, global_off, 0.0), axis=-1, keepdims=True) + r_k
        lslots = jnp.where(lane == k, ls, lslots)
        gslots = jnp.where(lane == k, gs.astype(I32), gslots)
    slot_ref[...] = gslots
    lt = lslots.T
    hb = hm_ref[...].astype(BF16)
    buf = i % 2
    for c in range(sorted_ref.shape[1] // chunk):
        pos = (c * chunk + lax.broadcasted_iota(I32, (chunk, tm), 0)).astype(F32)
        perm = jnp.zeros((chunk, tm), F32)
        for k in range(TOP_K):
            perm = perm + jnp.where(pos == lt[k:k + 1], 1.0, 0.0)
        sorted_ref[buf, c * chunk:(c + 1) * chunk, :] = _dot(perm.astype(BF16), hb)

    def copy(src, dst, rows, b=buf):
        src = pl.multiple_of(src, GRANULE)
        dst = pl.multiple_of(dst, GRANULE)
        return pltpu.make_async_copy(sorted_ref.at[b, pl.ds(src, rows)], xs_ref.at[pl.ds(dst, rows)], sem.at[b])

    big = 4 * GRANULE
    for e in range(N_EXPERTS):
        n = ngran_ref[i * N_EXPERTS + e]
        src0 = lstart_ref[i * N_EXPERTS + e]
        dst0 = gseg_ref[i * N_EXPERTS + e]
        n_big = lax.shift_right_logical(n, 2)
        rest0 = n_big * big

        def body_big(g, c, src0=src0, dst0=dst0):
            copy(src0 + g * big, dst0 + g * big, big).start()
            return c
        lax.fori_loop(0, n_big, body_big, 0)

        def body_small(g, c, src0=src0 + rest0, dst0=dst0 + rest0):
            copy(src0 + g * GRANULE, dst0 + g * GRANULE, GRANULE).start()
            return c
        lax.fori_loop(0, n - n_big * 4, body_small, 0)

    def wait_rows(rows, b):
        unit = 256
        lax.fori_loop(0, lax.shift_right_logical(rows, 8), lambda g, c: (copy(0, 0, unit, b).wait(), c)[1], 0)
        lax.fori_loop(0, lax.shift_right_logical(rows & (unit - 1), 3),
                      lambda g, c: (copy(0, 0, GRANULE, b).wait(), c)[1], 0)

    @pl.when(i > 0)
    def _():
        wait_rows(nrows_ref[i - 1], 1 - buf)

    @pl.when(i == pl.num_programs(0) - 1)
    def _():
        wait_rows(nrows_ref[i], buf)

    if fill_tail:
        @pl.when(i == pl.num_programs(0) - 1)
        def _():
            zeros_ref[...] = jnp.zeros_like(zeros_ref)
            end = total_ref[0]

            def fill(g):
                dst = pl.multiple_of(end + g * GRANULE, GRANULE)
                return pltpu.make_async_copy(zeros_ref, xs_ref.at[pl.ds(dst, GRANULE)], sem.at[buf])

            n_fill = (xs_ref.shape[0] - end) // GRANULE
            lax.fori_loop(0, n_fill, lambda g, c: (fill(g).start(), c)[1], 0)
            lax.fori_loop(0, n_fill, lambda g, c: (fill(g).wait(), c)[1], 0)


def _dispatch(route, tile0, hm, idx, rank, before, xs_prev, *, tm, fill_tail):
    t, d = hm.shape
    nt = t // tm
    n_local = TOP_K * tm + N_EXPERTS * GRANULE
    chunk = 256 if n_local % 256 == 0 else LANES
    assert n_local % chunk == 0
    has_prev = xs_prev is not None
    tiles = slice(tile0, tile0 + nt)
    flat = slice(tile0 * N_EXPERTS, (tile0 + nt) * N_EXPERTS)
    rowsp = lambda w: pl.BlockSpec((tm, w), lambda i, *_: (i, 0))
    tile3 = pl.BlockSpec((1, SUBLANES, LANES), lambda i, *_: (i, 0, 0))
    n_prefetch = 5
    in_specs = [rowsp(d), rowsp(LANES), rowsp(LANES), tile3, tile3, tile3]
    operands = [route["gseg"][flat], route["lstart"][flat], route["ngran"][flat], route["nrows"][tiles],
                route["total"], hm, idx, rank, before, route["lstart_v"][tiles], route["gseg_v"][tiles]]
    aliases = {}
    if has_prev:
        in_specs.append(pl.BlockSpec(memory_space=pl.ANY))
        aliases = {len(operands): 0}
        operands.append(xs_prev)
    grid_spec = pltpu.PrefetchScalarGridSpec(
        num_scalar_prefetch=n_prefetch, grid=(nt,), in_specs=in_specs,
        out_specs=[pl.BlockSpec(memory_space=pl.ANY), rowsp(LANES)],
        scratch_shapes=[pltpu.VMEM((2, n_local, d), F32), pltpu.VMEM((GRANULE, d), F32),
                        pltpu.SemaphoreType.DMA((2,))])
    return pl.pallas_call(
        functools.partial(_dispatch_kernel, chunk=chunk, has_prev=has_prev, fill_tail=fill_tail),
        grid_spec=grid_spec, input_output_aliases=aliases,
        out_shape=[jax.ShapeDtypeStruct((route["n_xs"], d), F32), jax.ShapeDtypeStruct((t, LANES), I32)],
        compiler_params=_params(("arbitrary",)), name="dispatch",
    )(*operands)


def _experts_kernel(ie_ref, ib_ref, lo_ref, hi_ref, fl_ref, ni_ref, xs_ref, wup_ref, bup_ref, wdn_ref, bdn_ref,
                    ys_ref, wup_b, wdn_b, *, d_ff):
    w = pl.program_id(0)
    bm = xs_ref.shape[0]

    @pl.when(w >= ni_ref[0])
    def _():
        ys_ref[...] = jnp.zeros_like(ys_ref)

    @pl.when(w < ni_ref[0])
    def _():
        first = (fl_ref[w] & 1) == 1

        @pl.when((fl_ref[w] & 2) == 2)
        def _():
            step = 128
            for r in range(0, wup_b.shape[0], step):
                wup_b[r:r + step, :] = wup_ref[0, r:r + step, :].astype(BF16)
            for r in range(0, wdn_b.shape[0], step):
                wdn_b[r:r + step, :] = wdn_ref[0, r:r + step, :].astype(BF16)

        x = xs_ref[...].astype(BF16)
        up = _dot(x, wup_b[...]) + bup_ref[0]
        gate = jnp.minimum(up[:, :d_ff], SWIGLU_LIMIT)
        lin = jnp.clip(up[:, d_ff:], -SWIGLU_LIMIT, SWIGLU_LIMIT)
        act = gate * jax.nn.sigmoid(SWIGLU_ALPHA * gate) * (lin + 1.0)
        out = _dot(act.astype(BF16), wdn_b[...]) + bdn_ref[0]

        @pl.when(first)
        def _():
            ys_ref[...] = out

        @pl.when(jnp.logical_not(first))
        def _():
            rowi = lax.broadcasted_iota(I32, (bm, 1), 0)
            mine = (rowi >= lo_ref[w]) & (rowi < hi_ref[w])
            ys_ref[...] = jnp.where(mine, out, ys_ref[...])


def _experts(route, xs, w_up, b_up, w_down, b_down, *, bm):
    n_xs, d = xs.shape
    n_work = route["item_expert"].shape[0]
    d_ff = w_down.shape[1]
    by_block = lambda w, ie, ib, lo, hi, fl, ni: (ib[w], 0)
    by_expert = lambda w, ie, ib, lo, hi, fl, ni: (ie[w], 0, 0)
    grid_spec = pltpu.PrefetchScalarGridSpec(
        num_scalar_prefetch=6, grid=(n_work,),
        in_specs=[pl.BlockSpec((bm, d), by_block),
                  pl.BlockSpec((1,) + w_up.shape[1:], by_expert), pl.BlockSpec((1, 1, b_up.shape[2]), by_expert),
                  pl.BlockSpec((1,) + w_down.shape[1:], by_expert), pl.BlockSpec((1, 1, d), by_expert)],
        out_specs=pl.BlockSpec((bm, d), by_block),
        scratch_shapes=[pltpu.VMEM(w_up.shape[1:], BF16), pltpu.VMEM(w_down.shape[1:], BF16)])
    return pl.pallas_call(
        functools.partial(_experts_kernel, d_ff=d_ff),
        grid_spec=grid_spec, out_shape=jax.ShapeDtypeStruct((n_xs, d), F32),
        compiler_params=_params(("arbitrary",)), name="experts",
    )(route["item_expert"], route["item_block"], route["item_lo"], route["item_hi"], route["item_flags"],
      route["n_items"], xs, w_up, b_up, w_down, b_down)


def _route(counts, before, *, t, bm):
    experts = jnp.arange(N_EXPERTS, dtype=I32)
    cnt = counts.astype(I32)
    bef = before.astype(I32)
    nt = bef.shape[0]
    local = jnp.concatenate([bef[1:], cnt[None]], axis=0) - bef
    ngran = (local + GRANULE - 1) // GRANULE
    lpad = ngran * GRANULE
    lend = jnp.cumsum(lpad, axis=1)
    lstart = lend - lpad
    tot = jnp.sum(lpad, axis=0)
    end = jnp.cumsum(tot)
    start = end - tot
    gseg = start[None] + jnp.cumsum(lpad, axis=0) - lpad
    max_rows = t * TOP_K + nt * N_EXPERTS * (GRANULE - 1)
    n_blocks = -(-max_rows // bm) + 1
    n_xs = n_blocks * bm
    n_work = n_blocks + N_EXPERTS - 1
    first_blk = start // bm
    n_touch = jnp.where(tot > 0, (end - 1) // bm - first_blk + 1, 0)
    wend = jnp.cumsum(n_touch)
    wstart = wend - n_touch
    n_items = wend[-1]
    w_all = jnp.arange(n_work, dtype=I32)
    w = jnp.minimum(w_all, n_items - 1)
    e = jnp.minimum(jnp.sum((w[:, None] >= wend[None, :]).astype(I32), axis=1), N_EXPERTS - 1)
    onehot = (e[:, None] == experts[None, :]).astype(I32)
    pick = lambda v: jnp.sum(onehot * v[None, :], axis=1)
    blk = pick(first_blk) + (w - pick(wstart))
    blk = jnp.minimum(blk + (w_all - w), n_blocks - 1)
    lo = jnp.maximum(pick(start) - blk * bm, 0)
    hi = jnp.minimum(pick(end) - blk * bm, bm)
    prev = lambda v: jnp.concatenate([jnp.full((1,), -1, I32), v[:-1]])
    flags = (blk != prev(blk)).astype(I32) + 2 * (e != prev(e)).astype(I32)
    lanes = lambda v: jnp.pad(v.astype(F32), ((0, 0), (0, LANES - N_EXPERTS)))
    return dict(item_expert=e, item_block=blk, item_lo=lo, item_hi=hi, item_flags=flags,
                n_items=n_items.reshape(1), gseg=gseg.reshape(-1), lstart=lstart.reshape(-1),
                ngran=ngran.reshape(-1), nrows=jnp.sum(lpad, axis=1), total=end[-1:], n_xs=n_xs,
                lstart_v=jnp.broadcast_to(lanes(lstart)[:, None, :], (nt, SUBLANES, LANES)),
                gseg_v=jnp.broadcast_to(lanes(gseg)[:, None, :], (nt, SUBLANES, LANES)))


def _tail_kernel(slot_ref, slotn_ref, x1_ref, ys_ref, gate_ref, p_ref, gple_ref, wg_ref, wp_ref, gfin_ref, o_ref,
                 ybuf0, ybuf1, sem):
    i = pl.program_id(0)
    last = pl.num_programs(0) - 1
    tm = x1_ref.shape[0]

    def row_copy(rows_ref, buf, b, k, r):
        return pltpu.make_async_copy(ys_ref.at[pl.ds(rows_ref[0, 0, k * tm + r], 1)], buf.at[k, pl.ds(r, 1)],
                                     sem.at[b])

    def wait_rows(buf, b):
        for k in range(TOP_K):
            pltpu.make_async_copy(ys_ref.at[pl.ds(0, tm)], buf.at[k], sem.at[b]).wait()

    @pl.when(i == 0)
    def _():
        for k in range(TOP_K):
            lax.fori_loop(0, tm, lambda r, c, k=k: (row_copy(slot_ref, ybuf0, 0, k, r).start(), c)[1], 0, unroll=8)

    def step(cur, nxt, bc, bn):
        wait_rows(cur, bc)
        n = 0
        for k in range(TOP_K):
            for r in range(tm):
                row_copy(slotn_ref, nxt, bn, k, r).start(priority=n % 2)
                n += 1
        gates = gate_ref[...]
        x2 = x1_ref[...]
        for k in range(TOP_K):
            x2 = x2 + gates[:, k:k + 1] * cur[k]
        g = jax.nn.sigmoid(_dot(_rms(x2, gple_ref[...]).astype(BF16), wg_ref[...]))
        x3 = x2 + _dot(p_ref[...].astype(BF16), wp_ref[...]) * g
        o_ref[...] = _rms(x3, gfin_ref[...])

        @pl.when(i == last)
        def _():
            wait_rows(nxt, bn)

    @pl.when(i % 2 == 0)
    def _():
        step(ybuf0, ybuf1, 0, 1)

    @pl.when(i % 2 == 1)
    def _():
        step(ybuf1, ybuf0, 1, 0)


def _tail(slots, x1, ys, gates, p, g_ple, w_gate, w_proj, g_final, *, tm):
    t, d = x1.shape
    nt = t // tm
    rowsp = lambda w: pl.BlockSpec((tm, w), lambda i: (i, 0))
    full = lambda a: pl.BlockSpec(a.shape, lambda i: (0, 0))
    smem = lambda f: pl.BlockSpec((1, 1, TOP_K * tm), f, memory_space=pltpu.SMEM)
    return pl.pallas_call(
        _tail_kernel, grid=(nt,),
        in_specs=[smem(lambda i: (i, 0, 0)), smem(lambda i: (jnp.minimum(i + 1, nt - 1), 0, 0)), rowsp(d),
                  pl.BlockSpec(memory_space=pl.ANY), rowsp(LANES), rowsp(p.shape[1])]
        + [full(a) for a in (g_ple, w_gate, w_proj, g_final)],
        out_specs=rowsp(d), out_shape=jax.ShapeDtypeStruct((t, d), F32),
        scratch_shapes=[pltpu.VMEM((TOP_K, tm, d), F32), pltpu.VMEM((TOP_K, tm, d), F32),
                        pltpu.SemaphoreType.DMA((2,))],
        compiler_params=_params(("arbitrary",)), name="tail",
    )(slots, slots, x1, ys, gates, p, g_ple, w_gate, w_proj, g_final)


def _block_diag(a, n):
    j, _, r, c = a.shape
    eye = jnp.eye(n, dtype=a.dtype)
    return jnp.einsum("ab,jarc->jarbc", eye, a).reshape(j, n * r, n * c)


def _mixer_tail(groups, lw, *, bm):
    cnt = jnp.zeros((SUBLANES, LANES), F32)
    routed = []
    for x_tok, attn, ssm_n, p_tok, tm in groups:
        tm = min(tm, x_tok.shape[0])
        x1, hm, idx, gates, rank, before, cnt = _outproj(
            x_tok, attn, ssm_n, lw["g_attn"], lw["wo_a"], lw["wo_s"], lw["g_ffn"], lw["wr_pad"], lw["br_pad"], cnt,
            tm=tm)
        routed.append((x1, hm, idx, gates, rank, before, p_tok, tm))
    before_all = jnp.concatenate([r[5][:, 0, :N_EXPERTS] for r in routed], axis=0)
    route = _route(cnt[0, :N_EXPERTS], before_all, t=sum(r[0].shape[0] for r in routed), bm=bm)
    xs, tile0, slots = None, 0, []
    for n, (x1, hm, idx, gates, rank, before, p_tok, tm) in enumerate(routed):
        nt = x1.shape[0] // tm
        xs, slot = _dispatch(route, tile0, hm, idx, rank, before, xs, tm=tm, fill_tail=n == len(routed) - 1)
        slots.append(jnp.transpose(slot[:, :TOP_K].reshape(nt, tm, TOP_K), (0, 2, 1)).reshape(nt, 1, TOP_K * tm))
        tile0 += nt
    ys = _experts(route, xs, lw["w_up"], lw["b_up"], lw["w_down"], lw["b_down"], bm=bm)
    return [_tail(s, x1, ys, gates, p_tok, lw["g_ple"], lw["w_ple_gate"], lw["w_ple_proj"], lw["g_final"], tm=tm)
            for s, (x1, hm, idx, gates, rank, before, p_tok, tm) in zip(slots, routed)]


def kernel(x_prompt, x_sample, p_prompt, p_sample, cache_k, cache_v, cache_logf, state_ssm_re, state_ssm_im,
           page_table, g_mix, w_in, b_fgate, lam_re, lam_im, log_step, b_ssm_re, b_ssm_im, c_ssm_re, c_ssm_im,
           d_ssm, w_glu, b_glu, g_attn_out, g_ssm_out, w_out, g_ffn, w_router, b_router, w_up, b_up, w_down,
           b_down, g_ple, w_ple_gate, w_ple_proj, g_final):
    depth = w_in.shape[0]
    assert depth == 1
    bp, sp, d = x_prompt.shape
    bd, sd, _ = x_sample.shape
    n_heads = b_fgate.shape[1]
    aw = n_heads * HEAD_DIM
    n_groups, n_state = lam_re.shape[1:]
    sw = n_groups * SSM_GROUP
    half = n_groups * n_state
    assert sw == SSM_LANE_BLOCKS * LANES and w_in.shape[2] == 3 * aw + n_heads + sw

    wi = w_in[0]
    wq, wk, wv = wi[:, :aw], wi[:, aw:2 * aw], wi[:, 2 * aw:3 * aw]
    wf, wu = wi[:, 3 * aw:3 * aw + n_heads], wi[:, 3 * aw + n_heads:]
    w_all = jnp.concatenate([wq, wk, wv, wu, jnp.pad(wf, ((0, 0), (0, LANES - n_heads)))], axis=1).astype(BF16)
    wft = jnp.pad(wf.T, ((0, 2 * SUBLANES - n_heads), (0, 0))).astype(BF16)
    bf_col = jnp.pad(b_fgate[0][:, None], ((0, 2 * SUBLANES - n_heads), (0, 0)))
    bf_row = jnp.pad(b_fgate, ((0, 0), (0, LANES - n_heads)))

    abar_re, abar_im, bbr_t, bbi_t = _s5_params(
        lam_re[0], lam_im[0], log_step[0], jnp.transpose(b_ssm_re[0], (0, 2, 1)), jnp.transpose(b_ssm_im[0], (0, 2, 1)))
    gpb = n_groups // SSM_LANE_BLOCKS
    blk4 = lambda a: a.reshape((SSM_LANE_BLOCKS, gpb) + a.shape[1:])
    bd_mat = jnp.concatenate([_block_diag(blk4(bbr_t), gpb), _block_diag(blk4(bbi_t), gpb)], axis=-1).astype(BF16)
    cdr = _block_diag(blk4(jnp.transpose(c_ssm_re[0], (0, 2, 1))), gpb).astype(BF16)
    cdi = _block_diag(blk4(jnp.transpose(c_ssm_im[0], (0, 2, 1))), gpb).astype(BF16)
    ssm_w = (abar_re.reshape(1, half), abar_im.reshape(1, half), bd_mat, cdr, cdi, d_ssm[0].reshape(1, sw),
             w_glu[0].astype(BF16), b_glu, g_ssm_out)

    lw = dict(g_attn=g_attn_out, wo_a=w_out[0, :aw].astype(BF16), wo_s=w_out[0, aw:].astype(BF16), g_ffn=g_ffn,
              wr_pad=jnp.pad(w_router[0], ((0, 0), (0, LANES - N_EXPERTS))).astype(BF16),
              br_pad=jnp.pad(b_router, ((0, 0), (0, LANES - N_EXPERTS))),
              w_up=w_up[0], b_up=b_up[0][:, None, :], w_down=w_down[0],
              b_down=b_down[0][:, None, :], g_ple=g_ple, w_ple_gate=w_ple_gate[0].astype(BF16),
              w_ple_proj=w_ple_proj[0].astype(BF16), g_final=g_final.reshape(1, d))

    q, k, v, kb, vb, u, lft, ct = _inproj_prompt(x_prompt, g_mix, w_all, wft, bf_col, aw=aw, n_heads=n_heads, tm=512)
    attn = _fox_prompt(q, kb, vb, ct, tq=512)
    n_time = min(64, sp)
    zeros_h = jnp.zeros((bp, half), F32)
    ssm_tb, hr_p, hi_p = _ssm(jnp.transpose(u, (1, 0, 2)).reshape(sp * bp, sw), zeros_h, zeros_h, *ssm_w,
                              n_batch=bp, n_time=n_time)
    ssm_p = jnp.transpose(ssm_tb.reshape(sp, bp, sw), (1, 0, 2)).reshape(bp * sp, sw)

    ts = bd * sd
    qs, ks, vs, us, lfs = _inproj_sample(x_sample.reshape(ts, d), g_mix, w_all, bf_row, aw=aw)
    lfs = lfs[:, :n_heads]
    kt_cache = jnp.transpose(cache_k[0], (0, 2, 3, 1))
    vt_cache = jnp.transpose(cache_v[0], (0, 2, 3, 1))
    lft_cache = jnp.transpose(cache_logf[0], (0, 2, 1))
    attn_s = _fox_decode(page_table, qs.reshape(bd, sd, aw), ks.reshape(bd, sd, aw), vs.reshape(bd, sd, aw),
                         jnp.transpose(lfs.reshape(bd, sd, n_heads), (0, 2, 1)), kt_cache, vt_cache, lft_cache,
                         n_group=32)
    ssm_s_tb, hr_s, hi_s = _ssm(jnp.transpose(us.reshape(bd, sd, sw), (1, 0, 2)).reshape(ts, sw),
                                state_ssm_re[0].reshape(bd, half), state_ssm_im[0].reshape(bd, half), *ssm_w,
                                n_batch=bd, n_time=sd)
    ssm_s = jnp.transpose(ssm_s_tb.reshape(sd, bd, sw), (1, 0, 2)).reshape(ts, sw)

    (y_p,) = _mixer_tail(
        [(x_prompt.reshape(bp * sp, d), attn.reshape(bp * sp, aw), ssm_p, p_prompt[0].reshape(bp * sp, -1), 512)],
        lw, bm=512)
    (y_s,) = _mixer_tail(
        [(x_sample.reshape(ts, d), attn_s.reshape(ts, aw), ssm_s, p_sample[0].reshape(ts, -1), 128)], lw, bm=128)

    heads = lambda a, b_, s_: a.reshape(1, b_, s_, n_heads, HEAD_DIM)
    state = lambda a, b_: a.reshape(1, b_, n_groups, n_state)
    return (y_p.reshape(bp, sp, d), y_s.reshape(bd, sd, d),
            heads(k, bp, sp), heads(v, bp, sp), jnp.transpose(lft, (0, 2, 1))[None],
            state(hr_p, bp), state(hi_p, bp),
            heads(ks, bd, sd), heads(vs, bd, sd), lfs.reshape(1, bd, sd, n_heads),
            state(hr_s, bd), state(hi_s, bd))
```
